```python
import functools
import jax, jax.numpy as jnp
from jax import lax
import numpy as np

D_MODEL = 1024
BATCH = 8
SEQ = 8192
DEPTH = 1
DEC_BATCH = 128
DEC_SEQ = 4
PAST_LEN = 8192
PAGE_SIZE = 128

N_HEADS = 8
HEAD_DIM = 64
KV_HEADS = 4
ATT_WIDTH = N_HEADS * HEAD_DIM
IDX_HEADS = 8
IDX_DIM = 64
TOPK_MAX = 256
Q_BLOCK = 128
HG_HEADS = 4
HG_KDIM = 128
HG_VDIM = 128
HG_KWIDTH = HG_HEADS * HG_KDIM
HG_VWIDTH = HG_HEADS * HG_VDIM
HG_CHUNK = 64
D_FF = 2816
ROPE_THETA = 10000.0
NORM_EPS = 1e-6

_SEGMENTS = (ATT_WIDTH, KV_HEADS * HEAD_DIM, KV_HEADS * HEAD_DIM,
             IDX_HEADS * IDX_DIM, IDX_DIM, IDX_HEADS,
             HG_KWIDTH, HG_KWIDTH, HG_VWIDTH, HG_VWIDTH,
             D_MODEL, D_MODEL)
IN_COLS = sum(_SEGMENTS)

kernel_name = 'hybrid_dsa_hgrn2_macaron_step'

F32 = jnp.float32


def _split_cols(p):
    out, start = [], 0
    for width in _SEGMENTS:
        out.append(p[..., start:start + width])
        start += width
    return out


def _rms_norm(x, g):
    xf = x.astype(F32)
    y = xf * lax.rsqrt(jnp.mean(xf * xf, axis=-1, keepdims=True) + NORM_EPS)
    return (y * g.astype(F32)).astype(x.dtype)


def _rope(x, pos):
    d = x.shape[-1]
    half = d // 2
    inv = ROPE_THETA ** (-jnp.arange(half, dtype=F32) * 2.0 / d)
    ang = pos.astype(F32)[:, None] * inv[None, :]
    cos = jnp.cos(ang)[None, :, None, :]
    sin = jnp.sin(ang)[None, :, None, :]
    xf = x.astype(F32)
    x1, x2 = xf[..., :half], xf[..., half:]
    return jnp.concatenate([x1 * cos - x2 * sin, x2 * cos + x1 * sin], axis=-1).astype(x.dtype)


def _swiglu(x, w_gate, w_up, w_down):
    return (jax.nn.silu(x @ w_gate) * (x @ w_up)) @ w_down


def _idx_scores(qi, wi, ki):
    s = jnp.einsum('bthd,bsd->bths', qi.astype(F32), ki.astype(F32)) * (IDX_DIM ** -0.5)
    return jnp.einsum('bths,bth->bts', jax.nn.relu(s), wi.astype(F32))


def _gather_rows(rows, idx):
    return jax.vmap(lambda r, i: r[i])(rows, idx)


def _sparse_attend(q, ks, vs, valid):
    b, t, h, d = q.shape
    qg = q.reshape(b, t, KV_HEADS, h // KV_HEADS, d)
    logits = jnp.einsum('btgrd,btkgd->btgrk', qg, ks).astype(F32) * (d ** -0.5)
    logits = jnp.where(valid[:, :, None, None, :], logits, -jnp.inf)
    p = jax.nn.softmax(logits, axis=-1).astype(vs.dtype)
    o = jnp.einsum('btgrk,btkgd->btgrd', p, vs)
    return o.reshape(b, t, h * d).astype(q.dtype)


def _dsa_prompt(q, k, v, qi, ki, wi):
    b, t, h, d = q.shape
    topk = min(TOPK_MAX, t // 4)
    nb = t // Q_BLOCK
    key_pos = jnp.arange(t)

    def blocks(a):
        return a.reshape((b, nb, Q_BLOCK) + a.shape[2:]).swapaxes(0, 1)

    def one_block(args):
        qb, qib, wib, pb = args
        sc = _idx_scores(qib, wib, ki)
        sc = jnp.where(key_pos[None, None, :] <= pb[None, :, None], sc, -jnp.inf)
        _, sel = lax.top_k(sc, topk)
        valid = sel <= pb[None, :, None]
        return _sparse_attend(qb, _gather_rows(k, sel), _gather_rows(v, sel), valid)

    out = lax.map(one_block, (blocks(q), blocks(qi), blocks(wi), key_pos.reshape(nb, Q_BLOCK)))
    return out.swapaxes(0, 1).reshape(b, t, h * d)


def _dsa_sample(q, k, v, qi, ki, wi, *, layer, cache_k, cache_v, cache_ik, page_table):
    db, tn = q.shape[:2]
    n_pages = page_table.shape[1]
    ps = cache_k.shape[2]
    past = n_pages * ps
    total = past + tn
    topk = min(TOPK_MAX, total // 4)
    ki_past = cache_ik[layer, page_table].reshape(db, past, IDX_DIM)
    ki_all = jnp.concatenate([ki_past.astype(ki.dtype), ki], axis=1)
    qpos = past + jnp.arange(tn)
    sc = _idx_scores(qi, wi, ki_all)
    sc = jnp.where(jnp.arange(total)[None, None, :] <= qpos[None, :, None], sc, -jnp.inf)
    _, sel = lax.top_k(sc, topk)
    valid = sel <= qpos[None, :, None]
    from_past = (sel < past)[..., None, None]
    sp = jnp.minimum(sel, past - 1)
    phys = jax.vmap(lambda pt, i: pt[i])(page_table, sp // ps)
    slot = sp % ps
    ns = jnp.clip(sel - past, 0, tn - 1)
    ks = jnp.where(from_past, cache_k[layer, phys, slot].astype(k.dtype), _gather_rows(k, ns))
    vs = jnp.where(from_past, cache_v[layer, phys, slot].astype(v.dtype), _gather_rows(v, ns))
    return _sparse_attend(q, ks, vs, valid)


def _gla_recurrence(q, k, v, logf, s0):
    b, t, h, kd = q.shape
    vd = v.shape[-1]
    c = HG_CHUNK if t % HG_CHUNK == 0 else t
    n = t // c

    def to_chunks(a):
        return a.astype(F32).reshape(b, n, c, h, a.shape[-1]).transpose(1, 0, 3, 2, 4)

    causal = jnp.tril(jnp.ones((c, c), dtype=bool))

    def step(S, inp):
        qc, kc, vc, gc = inp
        G = jnp.cumsum(gc, axis=2)
        o_inter = jnp.einsum('bhtk,bhkv->bhtv', qc * jnp.exp(G), S)
        diff = G[:, :, :, None, :] - G[:, :, None, :, :]
        decay = jnp.exp(jnp.where(causal[:, :, None], diff, -jnp.inf))
        A = jnp.einsum('bhtsk,bhsk->bhts', qc[:, :, :, None, :] * decay, kc)
        o = o_inter + jnp.einsum('bhts,bhsv->bhtv', A, vc)
        G_last = G[:, :, -1:, :]
        S = jnp.exp(G_last[:, :, 0, :])[..., None] * S + jnp.einsum('bhsk,bhsv->bhkv', kc * jnp.exp(G_last - G), vc)
        return S, o

    S, o = lax.scan(step, s0.astype(F32), (to_chunks(q), to_chunks(k), to_chunks(v), to_chunks(logf)))
    o = o.transpose(1, 0, 3, 2, 4).reshape(b, t, h, vd)
    return o, S


def _hgrn2_branch(hq, hf, hi, hg, lb, s0, hg_norm):
    b, t, _ = hq.shape
    q = jax.nn.silu(hq.astype(F32)).reshape(b, t, HG_HEADS, HG_KDIM) * (HG_KDIM ** -0.5)
    z = hf.astype(F32).reshape(b, t, HG_HEADS, HG_KDIM)
    lbh = lb.reshape(HG_HEADS, HG_KDIM)
    f = lbh + (1.0 - lbh) * jax.nn.sigmoid(z)
    k = (1.0 - lbh) * jax.nn.sigmoid(-z)
    v = hi.astype(F32).reshape(b, t, HG_HEADS, HG_VDIM)
    o, S = _gla_recurrence(q, k, v, jnp.log(f), s0)
    o = _rms_norm(o, hg_norm) * jax.nn.silu(hg.astype(F32).reshape(b, t, HG_HEADS, HG_VDIM))
    return o.reshape(b, t, HG_VWIDTH).astype(hq.dtype), S


def _layer(x, pos, attend, s0, lb, ffn1_norm, ffn1_w_gate, ffn1_w_up, ffn1_w_down, mix_norm, w_in,
           q_norm, k_norm, hg_norm, w_branch_attn, w_branch_hgrn, w_out,
           ffn2_norm, ffn2_w_gate, ffn2_w_up, ffn2_w_down):
    b, t, _ = x.shape
    x = x + 0.5 * _swiglu(_rms_norm(x, ffn1_norm), ffn1_w_gate, ffn1_w_up, ffn1_w_down)
    h = _rms_norm(x, mix_norm)
    q, k, v, qi, ki, wi, hq, hf, hi, hg, ga, gr = _split_cols(h @ w_in)
    q = _rope(_rms_norm(q.reshape(b, t, N_HEADS, HEAD_DIM), q_norm), pos)
    k = _rope(_rms_norm(k.reshape(b, t, KV_HEADS, HEAD_DIM), k_norm), pos)
    v = v.reshape(b, t, KV_HEADS, HEAD_DIM)
    qi = _rope(qi.reshape(b, t, IDX_HEADS, IDX_DIM), pos)
    ki = _rope(ki.reshape(b, t, 1, IDX_DIM), pos)[:, :, 0]
    wi = wi * (IDX_HEADS ** -0.5)
    att = attend(q, k, v, qi, ki, wi)
    rec, S = _hgrn2_branch(hq, hf, hi, hg, lb, s0, hg_norm)
    merged = jax.nn.sigmoid(ga) * (att @ w_branch_attn) + jax.nn.sigmoid(gr) * (rec @ w_branch_hgrn)
    x = x + merged @ w_out
    x = x + 0.5 * _swiglu(_rms_norm(x, ffn2_norm), ffn2_w_gate, ffn2_w_up, ffn2_w_down)
    return x, k, v, ki, S


def setup_inputs(seed: int = 0) -> dict:
    key = jax.random.key(seed)
    ks = jax.random.split(key, 26)
    n_pages = PAST_LEN // PAGE_SIZE
    n_used = DEC_BATCH * n_pages
    n_pool = n_used + max(1, n_used // 4)

    def nrm(k, shape, scale):
        return jax.random.normal(k, shape, F32) * scale

    def gain(k, shape):
        return 1.0 + 0.02 * jax.random.normal(k, shape, F32)

    page_table = jax.random.permutation(ks[6], n_pool)[:n_used].reshape(DEC_BATCH, n_pages).astype(jnp.int32)
    return {
        'x_prompt': nrm(ks[0], (BATCH, SEQ, D_MODEL), 1.0),
        'x_sample': nrm(ks[1], (DEC_BATCH, DEC_SEQ, D_MODEL), 1.0),
        'cache_k': nrm(ks[2], (DEPTH, n_pool, PAGE_SIZE, KV_HEADS, HEAD_DIM), 1.0),
        'cache_v': nrm(ks[3], (DEPTH, n_pool, PAGE_SIZE, KV_HEADS, HEAD_DIM), 1.0),
        'cache_idx_k': nrm(ks[4], (DEPTH, n_pool, PAGE_SIZE, IDX_DIM), 1.0),
        'state_hgrn': nrm(ks[5], (DEPTH, DEC_BATCH, HG_HEADS, HG_KDIM, HG_VDIM), 0.5),
        'page_table': page_table,
        'hg_lower_bound': nrm(ks[7], (DEPTH + 1, HG_KWIDTH), 0.1),
        'ffn1_norm': gain(ks[8], (DEPTH, D_MODEL)),
        'ffn1_w_gate': nrm(ks[9], (DEPTH, D_MODEL, D_FF), D_MODEL ** -0.5),
        'ffn1_w_up': nrm(ks[10], (DEPTH, D_MODEL, D_FF), D_MODEL ** -0.5),
        'ffn1_w_down': nrm(ks[11], (DEPTH, D_FF, D_MODEL), D_FF ** -0.5),
        'mix_norm': gain(ks[12], (DEPTH, D_MODEL)),
        'w_in': nrm(ks[13], (DEPTH, D_MODEL, IN_COLS), D_MODEL ** -0.5),
        'q_norm': gain(ks[14], (DEPTH, HEAD_DIM)),
        'k_norm': gain(ks[15], (DEPTH, HEAD_DIM)),
        'hg_norm': gain(ks[16], (DEPTH, HG_VDIM)),
        'w_branch_attn': nrm(ks[17], (DEPTH, ATT_WIDTH, D_MODEL), ATT_WIDTH ** -0.5),
        'w_branch_hgrn': nrm(ks[18], (DEPTH, HG_VWIDTH, D_MODEL), HG_VWIDTH ** -0.5),
        'w_out': nrm(ks[19], (DEPTH, D_MODEL, D_MODEL), D_MODEL ** -0.5),
        'ffn2_norm': gain(ks[20], (DEPTH, D_MODEL)),
        'ffn2_w_gate': nrm(ks[21], (DEPTH, D_MODEL, D_FF), D_MODEL ** -0.5),
        'ffn2_w_up': nrm(ks[22], (DEPTH, D_MODEL, D_FF), D_MODEL ** -0.5),
        'ffn2_w_down': nrm(ks[23], (DEPTH, D_FF, D_MODEL), D_FF ** -0.5),
    }


def reference(x_prompt, x_sample, cache_k, cache_v, cache_idx_k, state_hgrn, page_table, hg_lower_bound,
              ffn1_norm, ffn1_w_gate, ffn1_w_up, ffn1_w_down, mix_norm, w_in, q_norm, k_norm, hg_norm,
              w_branch_attn, w_branch_hgrn, w_out, ffn2_norm, ffn2_w_gate, ffn2_w_up, ffn2_w_down):
    lower_bounds = jnp.cumsum(jax.nn.softmax(hg_lower_bound.astype(F32), axis=0), axis=0)
    b, t = x_prompt.shape[:2]
    db, tn = x_sample.shape[:2]
    past_len = page_table.shape[1] * cache_k.shape[2]
    pos_prompt = jnp.arange(t)
    pos_sample = past_len + jnp.arange(tn)
    s0_prompt = jnp.zeros((b, HG_HEADS, HG_KDIM, HG_VDIM), F32)
    xp, xs = x_prompt, x_sample
    kp_l, vp_l, ikp_l, sp_l, ks_l, vs_l, iks_l, ss_l = [], [], [], [], [], [], [], []
    for l in range(DEPTH):
        w = (ffn1_norm[l], ffn1_w_gate[l], ffn1_w_up[l], ffn1_w_down[l], mix_norm[l], w_in[l],
             q_norm[l], k_norm[l], hg_norm[l], w_branch_attn[l], w_branch_hgrn[l], w_out[l],
             ffn2_norm[l], ffn2_w_gate[l], ffn2_w_up[l], ffn2_w_down[l])
        xp, kp, vp, ikp, sp = _layer(xp, pos_prompt, _dsa_prompt, s0_prompt, lower_bounds[l], *w)
        attend_sample = functools.partial(_dsa_sample, layer=l, cache_k=cache_k, cache_v=cache_v,
                                          cache_ik=cache_idx_k, page_table=page_table)
        xs, ksm, vsm, iksm, ssm = _layer(xs, pos_sample, attend_sample, state_hgrn[l], lower_bounds[l], *w)
        kp_l.append(kp); vp_l.append(vp); ikp_l.append(ikp); sp_l.append(sp.astype(x_prompt.dtype))
        ks_l.append(ksm); vs_l.append(vsm); iks_l.append(iksm); ss_l.append(ssm.astype(state_hgrn.dtype))
    return (xp, xs,
            jnp.stack(kp_l), jnp.stack(vp_l), jnp.stack(ikp_l), jnp.stack(sp_l),
            jnp.stack(ks_l), jnp.stack(vs_l), jnp.stack(iks_l), jnp.stack(ss_l))
```

```python
import functools

import numpy as np
import jax
import jax.numpy as jnp
from jax import lax
from jax.experimental import pallas as pl
from jax.experimental.pallas import tpu as pltpu

F32, BF16, I32 = jnp.float32, jnp.bfloat16, jnp.int32

N_HEADS, KV_HEADS, HEAD_DIM = 8, 4, 64
IDX_HEADS, IDX_DIM, TOPK_MAX = 8, 64, 256
HG_HEADS, HG_KDIM, HG_VDIM = 4, 128, 128
ATT_WIDTH = N_HEADS * HEAD_DIM
KV_WIDTH = KV_HEADS * HEAD_DIM
HG_WIDTH = HG_HEADS * HG_KDIM
ROPE_THETA = 10000.0
NORM_EPS = 1e-6

LANES = 128
MXU_DIM = 256
VMEM_LIMIT_BYTES = 56 * 2**20
INT_MIN = -2**31

ROW_TILE = 512
PROJ_ROW_TILE = 256
FFN_CHUNK = 256
HG_CHUNK = 64
HG_SUB = 16
Q_TILE = 128
K_CHUNK = 512


def _params(n_axes):
    return pltpu.CompilerParams(dimension_semantics=("arbitrary",) * n_axes,
                                vmem_limit_bytes=VMEM_LIMIT_BYTES)


def _resident(shape, n_axes):
    zeros = (0,) * len(shape)
    if n_axes == 1:
        return pl.BlockSpec(shape, lambda i: zeros, pipeline_mode=pl.Buffered(1))
    return pl.BlockSpec(shape, lambda i, j: zeros, pipeline_mode=pl.Buffered(1))


def _rms(x, gain):
    return x * lax.rsqrt(jnp.mean(x * x, axis=-1, keepdims=True) + NORM_EPS) * gain


def _sigmoid(x):
    return 1.0 / (1.0 + jnp.exp(-x))


def _dot(a, b):
    return jnp.dot(a, b, preferred_element_type=F32)


def _dot_nt(a, b):
    return lax.dot_general(a, b, (((1,), (1,)), ((), ())), preferred_element_type=F32)


def _dot_tn(a, b):
    return lax.dot_general(a, b, (((0,), (0,)), ((), ())), preferred_element_type=F32)


def _split2(x):
    hi = x.astype(BF16)
    return hi, (x - hi.astype(F32)).astype(BF16)


def _ffn_kernel(x_ref, g_ref, wg_ref, wu_ref, wd_ref, o_ref):
    x = x_ref[...]
    h = _rms(x, g_ref[...]).astype(BF16)
    d_ff = wg_ref.shape[1]
    acc = jnp.zeros(x.shape, F32)
    for c in range(d_ff // FFN_CHUNK):
        cols = slice(c * FFN_CHUNK, (c + 1) * FFN_CHUNK)
        g = _dot(h, wg_ref[:, cols])
        u = _dot(h, wu_ref[:, cols])
        a = (g * _sigmoid(g) * u).astype(BF16)
        acc = acc + _dot(a, wd_ref[cols, :])
    o_ref[...] = x + 0.5 * acc


def _ffn(x, norm, wg, wu, wd):
    n, d = x.shape
    tm = min(ROW_TILE, n)
    assert n % tm == 0 and wg.shape[1] % FFN_CHUNK == 0
    return pl.pallas_call(
        _ffn_kernel,
        grid=(n // tm,),
        in_specs=[pl.BlockSpec((tm, d), lambda i: (i, 0)), _resident((1, d), 1),
                  _resident(wg.shape, 1), _resident(wu.shape, 1), _resident(wd.shape, 1)],
        out_specs=pl.BlockSpec((tm, d), lambda i: (i, 0)),
        out_shape=jax.ShapeDtypeStruct((n, d), F32),
        compiler_params=_params(1),
        name="ffn",
    )(x, norm.reshape(1, d), wg, wu, wd)


def _rope(y, cos, sin):
    n = y.shape[1]
    reps = n // LANES
    c = jnp.tile(cos, (1, reps)) if reps > 1 else cos
    s = jnp.tile(sin, (1, reps)) if reps > 1 else sin
    lane = lax.broadcasted_iota(I32, y.shape, 1)
    first_half = (lane & (HEAD_DIM // 2)) == 0
    partner = jnp.where(first_half, pltpu.roll(y, n - HEAD_DIM // 2, axis=1),
                        pltpu.roll(y, HEAD_DIM // 2, axis=1))
    return y * c + partner * s


def _proj_kernel(x_ref, gn_ref, wq_ref, wk_ref, wv_ref, wqi_ref, wki_ref, wwi_ref, wh_ref, wg_ref,
                 bd_ref, pq_ref, qn_ref, kn_ref, cos_ref, sin_ref,
                 qx_ref, k_ref, v_ref, vb_ref, qc_ref, ki_ref, wi_ref, ph_ref, pg_ref, kb_ref, kc_ref,
                 *, transposed):
    h = _rms(x_ref[...], gn_ref[...]).astype(BF16)
    cos, sin = cos_ref[...], sin_ref[...]

    def head_norm(y, gain):
        n = y.shape[1]
        hi, lo = _split2(y * y)
        bd = bd_ref[:n, :n]
        ss = _dot(hi, bd) + _dot(lo, bd)
        return y * lax.rsqrt(ss * (1.0 / HEAD_DIM) + NORM_EPS) * gain

    q = _rope(head_norm(_dot(h, wq_ref[...]), qn_ref[...]), cos, sin) * (HEAD_DIM ** -0.5)
    qx_ref[...] = _dot(q.astype(BF16), pq_ref[...]).astype(BF16)

    k = _rope(head_norm(_dot(h, wk_ref[...]), kn_ref[...]), cos, sin)
    k_ref[...] = k
    v = _dot(h, wv_ref[...])
    v_ref[...] = v
    vb_ref[...] = v.astype(BF16)

    qi = _rope(_dot(h, wqi_ref[...]), cos, sin) * (IDX_DIM ** -0.5)
    lane = lax.broadcasted_iota(I32, qi.shape, 1) & (MXU_DIM - 1)
    is_lo = (lane >= IDX_DIM) & (lane < 2 * IDX_DIM)
    qc_ref[...] = jnp.where(is_lo, qi - qi.astype(BF16).astype(F32), qi).astype(BF16)

    ki = _rope(_dot(h, wki_ref[...]), cos, sin)
    ki_ref[...] = ki[:, :IDX_DIM]
    lane = lax.broadcasted_iota(I32, ki.shape, 1)
    is_lo = (lane >= 2 * IDX_DIM) & (lane < 3 * IDX_DIM)
    kc = jnp.where(is_lo, ki - ki.astype(BF16).astype(F32), ki)

    if transposed:
        kb_ref[0] = k.T.astype(BF16)
        kc_ref[0] = kc.T.astype(BF16)
    else:
        kb_ref[...] = k.astype(BF16)
        kc_ref[...] = kc.astype(BF16)

    wi_ref[...] = _dot(h, wwi_ref[...]) * (IDX_HEADS ** -0.5)
    ph_ref[...] = _dot(h, wh_ref[...])
    pg_ref[...] = _dot(h, wg_ref[...])


def _proj(x, batch, seq, pos0, w, transposed):
    n, d = x.shape
    tm = min(PROJ_ROW_TILE, n)
    assert n % tm == 0
    lane = np.arange(LANES)
    inv = ROPE_THETA ** (-(lane % (HEAD_DIM // 2)).astype(np.float32) * 2.0 / HEAD_DIM)
    sign = np.where((lane % HEAD_DIM) < HEAD_DIM // 2, -1.0, 1.0).astype(np.float32)
    ang = (pos0 + jnp.arange(seq)).astype(F32)[:, None] * jnp.asarray(inv)[None, :]
    cos_t, sin_t = jnp.cos(ang), jnp.sin(ang) * jnp.asarray(sign)[None, :]
    if seq >= tm:
        assert seq % tm == 0
        per_seq = seq // tm
        tab_map = lambda i: (i % per_seq, 0)
    else:
        assert tm % seq == 0
        cos_t, sin_t = jnp.tile(cos_t, (tm // seq, 1)), jnp.tile(sin_t, (tm // seq, 1))
        tab_map = lambda i: (0, 0)

    row = lambda width: pl.BlockSpec((tm, width), lambda i: (i, 0))
    if transposed:
        per_seq = seq // tm
        kt_spec = pl.BlockSpec((1, MXU_DIM, tm), lambda i: (i // per_seq, 0, i % per_seq))
        kt_shape = jax.ShapeDtypeStruct((batch, MXU_DIM, seq), BF16)
    else:
        kt_spec = row(MXU_DIM)
        kt_shape = jax.ShapeDtypeStruct((n, MXU_DIM), BF16)

    weights = [w["wq"], w["wk"], w["wv"], w["wqi"], w["wki"], w["wwi"], w["wh"], w["wg"],
               w["bd"], w["pq"], w["qn"], w["kn"]]
    out_shape = [
        jax.ShapeDtypeStruct((n, N_HEADS * MXU_DIM), BF16),
        jax.ShapeDtypeStruct((n, KV_WIDTH), F32),
        jax.ShapeDtypeStruct((n, KV_WIDTH), F32),
        jax.ShapeDtypeStruct((n, KV_WIDTH), BF16),
        jax.ShapeDtypeStruct((n, IDX_HEADS * MXU_DIM), BF16),
        jax.ShapeDtypeStruct((n, IDX_DIM), F32),
        jax.ShapeDtypeStruct((n, LANES), F32),
        jax.ShapeDtypeStruct((n, 4 * HG_WIDTH), F32),
        jax.ShapeDtypeStruct((n, 2 * d), F32),
        kt_shape,
        kt_shape,
    ]
    out_specs = [row(N_HEADS * MXU_DIM), row(KV_WIDTH), row(KV_WIDTH), row(KV_WIDTH),
                 row(IDX_HEADS * MXU_DIM), row(IDX_DIM), row(LANES), row(4 * HG_WIDTH), row(2 * d),
                 kt_spec, kt_spec]
    return pl.pallas_call(
        functools.partial(_proj_kernel, transposed=transposed),
        grid=(n // tm,),
        in_specs=[row(d), _resident((1, d), 1)] + [_resident(a.shape, 1) for a in weights]
                 + [pl.BlockSpec((tm, LANES), tab_map), pl.BlockSpec((tm, LANES), tab_map)],
        out_specs=out_specs,
        out_shape=out_shape,
        compiler_params=_params(1),
        name="proj",
    )(x, w["mix_norm"], *weights, cos_t, sin_t)


def _hgrn_kernel(ph_ref, lbp_ref, s0_ref, gn_ref, tri_ref, rec_ref, s_ref, st_ref, o_ref,
                 *, layer, chunk, sub, t_valid):
    j = pl.program_id(1)

    @pl.when(j == 0)
    def _():
        for hh in range(HG_HEADS):
            st_ref[hh] = s0_ref[0, hh].T

    lbp = lbp_ref[...]
    e = jnp.exp(lbp - jnp.max(lbp, axis=0, keepdims=True))
    lb_all = jnp.sum(e[:layer + 1], axis=0, keepdims=True) / jnp.sum(e, axis=0, keepdims=True)

    tri = tri_ref[...]
    row_c = lax.broadcasted_iota(I32, (chunk, HG_KDIM), 0)
    row_s = lax.broadcasted_iota(I32, (sub, HG_KDIM), 0)
    valid = (j * chunk + row_c) < t_valid
    w = HG_WIDTH
    for hh in range(HG_HEADS):
        lanes = slice(hh * HG_KDIM, (hh + 1) * HG_KDIM)
        hq = ph_ref[0, :, hh * HG_KDIM:(hh + 1) * HG_KDIM]
        z = ph_ref[0, :, w + hh * HG_KDIM:w + (hh + 1) * HG_KDIM]
        v = ph_ref[0, :, 2 * w + hh * HG_VDIM:2 * w + (hh + 1) * HG_VDIM]
        hg = ph_ref[0, :, 3 * w + hh * HG_VDIM:3 * w + (hh + 1) * HG_VDIM]
        lb = lb_all[:, lanes]
        q = hq * _sigmoid(hq) * (HG_KDIM ** -0.5)
        logf = jnp.where(valid, jnp.log(lb + (1.0 - lb) * _sigmoid(z)), 0.0)
        kk = jnp.where(valid, (1.0 - lb) * _sigmoid(-z), 0.0)

        l1 = logf.astype(BF16)
        r1 = logf - l1.astype(F32)
        l2 = r1.astype(BF16)
        l3 = (r1 - l2.astype(F32)).astype(BF16)
        g = _dot(tri, l1) + _dot(tri, l2) + _dot(tri, l3)
        g_last = g[chunk - 1:chunk, :]

        st = st_ref[hh]
        vb = v.astype(BF16)
        o_ref[hh] = _dot_nt((q * jnp.exp(g)).astype(BF16), st.astype(BF16))
        for i in range(chunk // sub):
            r0 = i * sub
            gi, qi, ki_, vi = g[r0:r0 + sub], q[r0:r0 + sub], kk[r0:r0 + sub], v[r0:r0 + sub]
            if i > 0:
                ref = g[r0 - 1:r0, :]
                q_rel = (qi * jnp.exp(gi - ref)).astype(BF16)
                k_rel = (kk[:r0] * jnp.exp(ref - g[:r0])).astype(BF16)
                a = _dot_nt(q_rel, k_rel)
                o_ref[hh, r0:r0 + sub, :] += _dot(a.astype(BF16), vb[:r0])
            for t in range(sub):
                dec = jnp.exp(jnp.where(row_s <= t, gi[t:t + 1, :] - gi, -jnp.inf))
                a_t = jnp.sum(qi[t:t + 1, :] * dec * ki_, axis=1, keepdims=True)
                o_ref[hh, r0 + t:r0 + t + 1, :] += jnp.sum(a_t * vi, axis=0, keepdims=True)

        k_end = (kk * jnp.exp(g_last - g)).astype(BF16)
        st_ref[hh] = jnp.exp(g_last) * st + _dot_tn(vb, k_end)

        o = o_ref[hh]
        gate = hg * _sigmoid(hg)
        rec_ref[0, :, hh * HG_VDIM:(hh + 1) * HG_VDIM] = (_rms(o, gn_ref[...]) * gate).astype(BF16)

    @pl.when(j == pl.num_programs(1) - 1)
    def _():
        for hh in range(HG_HEADS):
            s_ref[0, hh] = st_ref[hh].T


def _hgrn(ph, lb_param, s0, hg_norm, layer, t_valid):
    b, t, _ = ph.shape
    chunk = min(HG_CHUNK, t)
    sub = min(HG_SUB, chunk)
    assert t % chunk == 0 and chunk % sub == 0
    tri = jnp.asarray(np.tril(np.ones((chunk, chunk), np.float32)), BF16)
    return pl.pallas_call(
        functools.partial(_hgrn_kernel, layer=layer, chunk=chunk, sub=sub, t_valid=t_valid),
        grid=(b, t // chunk),
        in_specs=[pl.BlockSpec((1, chunk, 4 * HG_WIDTH), lambda i, j: (i, j, 0)),
                  _resident(lb_param.shape, 2),
                  pl.BlockSpec((1, HG_HEADS, HG_KDIM, HG_VDIM), lambda i, j: (i, 0, 0, 0)),
                  _resident((1, HG_VDIM), 2), _resident(tri.shape, 2)],
        out_specs=[pl.BlockSpec((1, chunk, HG_WIDTH), lambda i, j: (i, j, 0)),
                   pl.BlockSpec((1, HG_HEADS, HG_KDIM, HG_VDIM), lambda i, j: (i, 0, 0, 0))],
        out_shape=[jax.ShapeDtypeStruct((b, t, HG_WIDTH), BF16),
                   jax.ShapeDtypeStruct((b, HG_HEADS, HG_KDIM, HG_VDIM), F32)],
        scratch_shapes=[pltpu.VMEM((HG_HEADS, HG_VDIM, HG_KDIM), F32),
                        pltpu.VMEM((HG_HEADS, chunk, HG_VDIM), F32)],
        compiler_params=_params(2),
        name="hgrn",
    )(ph, lb_param, s0, hg_norm.reshape(1, HG_VDIM), tri)


def _sortable_key(score):
    bits = pltpu.bitcast(score, I32)
    return jnp.where(bits < 0, INT_MIN - bits, bits)


def _kth_largest(count_ge, topk, shape):
    tau = jnp.full(shape, INT_MIN, I32)
    zero = jnp.zeros(shape, I32)
    tau = jnp.where(count_ge(zero) >= topk, zero, tau)

    def step(it, tau):
        cand = tau | lax.shift_left(jnp.int32(1), 30 - it)
        return jnp.where(count_ge(cand) >= topk, cand, tau)

    return lax.fori_loop(0, 31, step, tau)


def _dsa_prompt_kernel(qx_ref, qc_ref, wi_ref, kct_ref, kt_ref, v_ref, tri_ref, sel_ref, o_ref,
                       keys_ref, aq_ref, ac_ref, wb_ref, m_ref, l_ref, acc_ref, *, tq, ck, topk):
    i = pl.program_id(1)
    row0 = i * tq
    n_chunks = (row0 + tq + ck - 1) // ck
    n_sub = ck // LANES
    rows_q = N_HEADS * tq

    for h in range(IDX_HEADS):
        ac_ref[h * tq:(h + 1) * tq, :] = qc_ref[0, :, h * MXU_DIM:(h + 1) * MXU_DIM]
        aq_ref[h * tq:(h + 1) * tq, :] = qx_ref[0, :, h * MXU_DIM:(h + 1) * MXU_DIM]
        wb_ref[h] = jnp.broadcast_to(wi_ref[0, :, h:h + 1], (tq, LANES))

    row_id = row0 + lax.broadcasted_iota(I32, (tq, ck), 0)
    col_id = lax.broadcasted_iota(I32, (tq, ck), 1)

    def score_body(c, carry):
        col0 = pl.multiple_of(c * ck, ck)
        s = _dot(ac_ref[...], kct_ref[0, :, pl.ds(col0, ck)])
        tot = jnp.zeros((tq, ck), F32)
        for h in range(IDX_HEADS):
            tot = tot + jnp.maximum(s[h * tq:(h + 1) * tq, :], 0.0) * jnp.tile(wb_ref[h], (1, n_sub))
        keys_ref[:, pl.ds(col0, ck)] = jnp.where(col0 + col_id <= row_id, _sortable_key(tot), INT_MIN)
        return carry

    lax.fori_loop(0, n_chunks, score_body, 0)

    def count(pred):
        def body(c, cnt):
            col0 = pl.multiple_of(c * ck, ck)
            blk = keys_ref[:, pl.ds(col0, ck)]
            for u in range(n_sub):
                cnt = cnt + jnp.where(pred(blk[:, u * LANES:(u + 1) * LANES]), 1, 0)
            return cnt
        cnt = lax.fori_loop(0, n_chunks, body, jnp.zeros((tq, LANES), I32))
        return jnp.sum(cnt, axis=1, keepdims=True)

    tau = _kth_largest(lambda cand: count(lambda blk: blk >= cand), topk, (tq, LANES))
    tau_sel = jnp.maximum(tau, INT_MIN + 1)
    n_gt = count(lambda blk: blk > tau_sel)
    n_ge = count(lambda blk: blk >= tau_sel)
    need_eq = jnp.where(n_ge >= topk, topk - n_gt, n_ge).astype(F32)
    has_ties = jnp.max(jnp.where(n_ge >= topk, n_ge - topk, 0)) > 0

    m_ref[...] = jnp.full(m_ref.shape, -jnp.inf, F32)
    l_ref[...] = jnp.zeros(l_ref.shape, F32)
    acc_ref[...] = jnp.zeros(acc_ref.shape, F32)

    def attend(ranked):
        def body(c, eq_seen):
            col0 = pl.multiple_of(c * ck, ck)
            blk = keys_ref[:, pl.ds(col0, ck)]
            tau_t = jnp.tile(tau_sel, (1, n_sub))
            if ranked:
                eq = blk == tau_t
                eq_f = jnp.where(eq, 1.0, 0.0)
                before = _dot(eq_f.astype(BF16), tri_ref[...]) + eq_seen
                rank = jnp.where(eq, before, jnp.where(blk > tau_t, -1.0, jnp.inf))
                mask = rank < need_eq
                eq_seen = eq_seen + jnp.sum(eq_f, axis=1, keepdims=True)
            else:
                mask = blk >= tau_t
            logits = _dot(aq_ref[...], kt_ref[0, :, pl.ds(col0, ck)])
            vb = v_ref[0, pl.ds(col0, ck), :]
            for h in range(N_HEADS):
                rows = slice(h * tq, (h + 1) * tq)
                lg = jnp.where(mask, logits[rows, :], -jnp.inf)
                m_old = m_ref[rows, :]
                m_new = jnp.maximum(m_old, jnp.max(lg, axis=1, keepdims=True))
                m_safe = jnp.where(m_new == -jnp.inf, 0.0, m_new)
                alpha = jnp.exp(m_old - m_safe)
                p = jnp.exp(lg - m_safe)
                l_ref[rows, :] = alpha * l_ref[rows, :] + jnp.sum(p, axis=1, keepdims=True)
                acc_ref[rows, :] = alpha * acc_ref[rows, :] + _dot(p.astype(BF16), vb)
                m_ref[rows, :] = m_new
            return eq_seen
        lax.fori_loop(0, n_chunks, body, jnp.zeros((tq, 1), F32))

    @pl.when(has_ties)
    def _():
        attend(True)

    @pl.when(jnp.logical_not(has_ties))
    def _():
        attend(False)

    out = jnp.zeros((tq, ATT_WIDTH), F32)
    for h in range(N_HEADS):
        rows = slice(h * tq, (h + 1) * tq)
        o_h = acc_ref[rows, :] / l_ref[rows, :]
        out = out + _dot(o_h.astype(BF16), sel_ref[h])
    o_ref[0] = out.astype(BF16)


def _dsa_prompt(qx, qc, wi, kct, kt, vb, sel):
    b, t, _ = qx.shape
    tq, ck = min(Q_TILE, t), min(K_CHUNK, t)
    assert t % tq == 0 and t % ck == 0 and ck % tq == 0
    topk = min(TOPK_MAX, t // 4)
    tri = jnp.asarray(np.triu(np.ones((ck, ck), np.float32), 1), BF16)
    tile = lambda width: pl.BlockSpec((1, tq, width), lambda i, j: (i, j, 0))
    whole = lambda rows, cols: pl.BlockSpec((1, rows, cols), lambda i, j: (i, 0, 0))
    return pl.pallas_call(
        functools.partial(_dsa_prompt_kernel, tq=tq, ck=ck, topk=topk),
        grid=(b, t // tq),
        in_specs=[tile(N_HEADS * MXU_DIM), tile(IDX_HEADS * MXU_DIM), tile(LANES),
                  whole(MXU_DIM, t), whole(MXU_DIM, t), whole(t, KV_WIDTH),
                  _resident(tri.shape, 2), _resident(sel.shape, 2)],
        out_specs=tile(ATT_WIDTH),
        out_shape=jax.ShapeDtypeStruct((b, t, ATT_WIDTH), BF16),
        scratch_shapes=[pltpu.VMEM((tq, t), I32),
                        pltpu.VMEM((N_HEADS * tq, MXU_DIM), BF16),
                        pltpu.VMEM((IDX_HEADS * tq, MXU_DIM), BF16),
                        pltpu.VMEM((IDX_HEADS, tq, LANES), F32),
                        pltpu.VMEM((N_HEADS * tq, 1), F32),
                        pltpu.VMEM((N_HEADS * tq, 1), F32),
                        pltpu.VMEM((N_HEADS * tq, KV_WIDTH), F32)],
        compiler_params=_params(2),
        name="dsa_prompt",
    )(qx, qc, wi, kct, kt, vb, tri, sel)


def _dsa_sample_kernel(pt_ref, qx_ref, qc_ref, wb_ref, kcn_ref, kn_ref, vn_ref, place_ref, tri_ref,
                       cik_hbm, ck_hbm, cv_hbm, o_ref,
                       ik_buf, k_buf, v_buf, keys_ref, sem,
                       *, layer, n_pages, page, tn, topk):
    b = pl.program_id(0)
    nb = pl.num_programs(0)
    past = n_pages * page
    width = past + LANES
    rows = tn * N_HEADS

    def copies(sample, slot, p):
        pg = pt_ref[sample, p]
        dst = pl.ds(p * page, page)
        return (pltpu.make_async_copy(cik_hbm.at[layer, pg], ik_buf.at[slot, dst], sem.at[slot, 0]),
                pltpu.make_async_copy(ck_hbm.at[layer, pg], k_buf.at[slot, dst], sem.at[slot, 1]),
                pltpu.make_async_copy(cv_hbm.at[layer, pg], v_buf.at[slot, dst], sem.at[slot, 2]))

    def fetch(sample, slot):
        def body(p, carry):
            for cp in copies(sample, slot, p):
                cp.start()
            return carry
        lax.fori_loop(0, n_pages, body, 0)

    def wait(sample, slot):
        def body(p, carry):
            for cp in copies(sample, slot, p):
                cp.wait()
            return carry
        lax.fori_loop(0, n_pages, body, 0)

    slot = b % 2

    @pl.when(b == 0)
    def _():
        fetch(0, 0)

    @pl.when(b + 1 < nb)
    def _():
        fetch(b + 1, 1 - slot)

    wait(b, slot)

    qc = qc_ref[0]
    qx = qx_ref[0]
    wb = wb_ref[0]
    tok = lax.broadcasted_iota(I32, (tn, LANES), 0)
    col = lax.broadcasted_iota(I32, (tn, LANES), 1)

    def head_sum(s):
        n = s.shape[1]
        r = jnp.maximum(s, 0.0) * jnp.tile(wb, (1, n // LANES))
        return jnp.sum(r.reshape(tn, N_HEADS, n), axis=1)

    keys_ref[...] = jnp.full(keys_ref.shape, INT_MIN, I32)

    def score_body(p, carry):
        r0 = pl.multiple_of(p * page, page)
        hi, lo = _split2(ik_buf[slot, pl.ds(r0, page), :])
        kc = (_dot(hi, place_ref[0]) + _dot(lo, place_ref[1])).astype(BF16)
        keys_ref[0:tn, pl.ds(r0, page)] = _sortable_key(head_sum(_dot_nt(qc, kc)))
        return carry

    lax.fori_loop(0, n_pages, score_body, 0)
    s_new = head_sum(_dot_nt(qc, kcn_ref[0]))
    keys_ref[0:tn, past:width] = jnp.where(col <= tok, _sortable_key(s_new), INT_MIN)

    n_blk = width // LANES

    def count(pred):
        def body(u, cnt):
            c0 = pl.multiple_of(u * LANES, LANES)
            return cnt + jnp.where(pred(keys_ref[:, pl.ds(c0, LANES)]), 1, 0)
        cnt = lax.fori_loop(0, n_blk, body, jnp.zeros((8, LANES), I32))
        return jnp.sum(cnt, axis=1, keepdims=True)

    tau = _kth_largest(lambda cand: count(lambda blk: blk >= cand), topk, (8, LANES))
    tau_sel = jnp.maximum(tau, INT_MIN + 1)
    n_gt = count(lambda blk: blk > tau_sel)
    n_ge = count(lambda blk: blk >= tau_sel)
    need_eq = jnp.where(n_ge >= topk, topk - n_gt, n_ge).astype(F32)

    def expand(x):
        n = x.shape[1]
        return jnp.broadcast_to(x[0:tn].reshape(tn, 1, n), (tn, N_HEADS, n)).reshape(rows, n)

    def select(blk, eq_seen):
        eq = blk == tau_sel
        eq_f = jnp.where(eq, 1.0, 0.0)
        before = _dot(eq_f.astype(BF16), tri_ref[...]) + eq_seen
        rank = jnp.where(eq, before, jnp.where(blk > tau_sel, -1.0, jnp.inf))
        return expand(rank) < expand(need_eq), eq_seen + jnp.sum(eq_f, axis=1, keepdims=True)

    def attend_body(p, carry):
        m_old, l_old, acc, eq_seen = carry
        r0 = pl.multiple_of(p * page, page)
        mask, eq_seen = select(keys_ref[:, pl.ds(r0, page)], eq_seen)
        kb = k_buf[slot, pl.ds(r0, page), :].astype(BF16)
        vb = v_buf[slot, pl.ds(r0, page), :].astype(BF16)
        lg = jnp.where(mask, _dot_nt(qx, kb), -jnp.inf)
        m_new = jnp.maximum(m_old, jnp.max(lg, axis=1, keepdims=True))
        m_safe = jnp.where(m_new == -jnp.inf, 0.0, m_new)
        alpha = jnp.exp(m_old - m_safe)
        pr = jnp.exp(lg - m_safe)
        l_new = alpha * l_old + jnp.sum(pr, axis=1, keepdims=True)
        acc = alpha * acc + _dot(pr.astype(BF16), vb)
        return m_new, l_new, acc, eq_seen

    init = (jnp.full((rows, 1), -jnp.inf, F32), jnp.zeros((rows, 1), F32),
            jnp.zeros((rows, KV_WIDTH), F32), jnp.zeros((8, 1), F32))
    m_old, l_old, acc, eq_seen = lax.fori_loop(0, n_pages, attend_body, init)

    mask, _ = select(keys_ref[:, past:width], eq_seen)
    lg = jnp.where(mask, _dot_nt(qx, kn_ref[0]), -jnp.inf)
    m_new = jnp.maximum(m_old, jnp.max(lg, axis=1, keepdims=True))
    alpha = jnp.exp(m_old - m_new)
    pr = jnp.exp(lg - m_new)
    l_new = alpha * l_old + jnp.sum(pr, axis=1, keepdims=True)
    acc = alpha * acc + _dot(pr.astype(BF16), vn_ref[0])
    o_ref[0] = acc / l_new


def _dsa_sample(page_table, qx, qc, wb, kcn, kn, vn, cache_ik, cache_k, cache_v, layer):
    db, rows, _ = qx.shape
    tn = rows // N_HEADS
    n_pages = page_table.shape[1]
    page = cache_ik.shape[2]
    past = n_pages * page
    assert page == LANES and tn <= 8
    tri = jnp.asarray(np.triu(np.ones((page, page), np.float32), 1), BF16)
    topk = min(TOPK_MAX, (past + tn) // 4)
    place = np.zeros((2, IDX_DIM, MXU_DIM), np.float32)
    eye = np.eye(IDX_DIM, dtype=np.float32)
    place[0, :, 0:IDX_DIM] = eye
    place[0, :, IDX_DIM:2 * IDX_DIM] = eye
    place[1, :, 2 * IDX_DIM:3 * IDX_DIM] = eye
    place = jnp.asarray(place, BF16)
    per = lambda r, c: pl.BlockSpec((1, r, c), lambda i, pt: (i, 0, 0))
    grid_spec = pltpu.PrefetchScalarGridSpec(
        num_scalar_prefetch=1,
        grid=(db,),
        in_specs=[per(rows, MXU_DIM), per(rows, MXU_DIM), per(rows, LANES),
                  per(LANES, MXU_DIM), per(LANES, MXU_DIM), per(LANES, KV_WIDTH),
                  pl.BlockSpec(place.shape, lambda i, pt: (0, 0, 0)),
                  pl.BlockSpec(tri.shape, lambda i, pt: (0, 0)),
                  pl.BlockSpec(memory_space=pl.ANY), pl.BlockSpec(memory_space=pl.ANY),
                  pl.BlockSpec(memory_space=pl.ANY)],
        out_specs=per(rows, KV_WIDTH),
        scratch_shapes=[pltpu.VMEM((2, past, IDX_DIM), F32),
                        pltpu.VMEM((2, past, KV_WIDTH), F32),
                        pltpu.VMEM((2, past, KV_WIDTH), F32),
                        pltpu.VMEM((8, past + LANES), I32),
                        pltpu.SemaphoreType.DMA((2, 3))],
    )
    return pl.pallas_call(
        functools.partial(_dsa_sample_kernel, layer=layer, n_pages=n_pages, page=page, tn=tn, topk=topk),
        grid_spec=grid_spec,
        out_shape=jax.ShapeDtypeStruct((db, rows, KV_WIDTH), F32),
        compiler_params=_params(1),
        name="dsa_sample",
    )(page_table, qx, qc, wb, kcn, kn, vn, place, tri, cache_ik, cache_k, cache_v)


def _merge_kernel(x_ref, att_ref, rec_ref, pg_ref, wa_ref, wh_ref, wo_ref, o_ref):
    d = x_ref.shape[1]
    pg = pg_ref[...]
    merged = (_sigmoid(pg[:, :d]) * _dot(att_ref[...], wa_ref[...])
              + _sigmoid(pg[:, d:]) * _dot(rec_ref[...], wh_ref[...]))
    o_ref[...] = x_ref[...] + _dot(merged.astype(BF16), wo_ref[...])


def _merge(x, att, rec, pg, wa, wh, wo):
    n, d = x.shape
    tm = min(ROW_TILE, n)
    row = lambda width: pl.BlockSpec((tm, width), lambda i: (i, 0))
    return pl.pallas_call(
        _merge_kernel,
        grid=(n // tm,),
        in_specs=[row(d), row(ATT_WIDTH), row(HG_WIDTH), row(2 * d),
                  _resident(wa.shape, 1), _resident(wh.shape, 1), _resident(wo.shape, 1)],
        out_specs=row(d),
        out_shape=jax.ShapeDtypeStruct((n, d), F32),
        compiler_params=_params(1),
        name="merge",
    )(x, att, rec, pg, wa, wh, wo)


def _prep_layer(l, ffn1_norm, ffn1_w_gate, ffn1_w_up, ffn1_w_down, mix_norm, w_in, q_norm, k_norm, hg_norm,
                w_branch_attn, w_branch_hgrn, w_out, ffn2_norm, ffn2_w_gate, ffn2_w_up, ffn2_w_down):
    d = w_in.shape[1]
    w = w_in[l]
    off = [0]

    def take(width):
        a = w[:, off[0]:off[0] + width]
        off[0] += width
        return a

    wq, wk, wv = take(ATT_WIDTH), take(KV_WIDTH), take(KV_WIDTH)
    wqi, wki, wwi = take(IDX_HEADS * IDX_DIM), take(IDX_DIM), take(IDX_HEADS)
    wh = take(4 * HG_WIDTH)
    wg = take(2 * d)
    zeros = jnp.zeros((d, IDX_DIM), F32)
    wqi_heads = wqi.reshape(d, IDX_HEADS, IDX_DIM)
    wqi_arr = jnp.concatenate([wqi_heads, wqi_heads, wqi_heads, jnp.zeros_like(wqi_heads)], axis=2)
    wki_arr = jnp.concatenate([wki, wki, wki, zeros], axis=1)
    wwi_arr = jnp.concatenate([wwi, jnp.zeros((d, LANES - IDX_HEADS), F32)], axis=1)
    bd = np.kron(np.eye(ATT_WIDTH // HEAD_DIM, dtype=np.float32), np.ones((HEAD_DIM, HEAD_DIM), np.float32))
    pq = np.zeros((ATT_WIDTH, N_HEADS * MXU_DIM), np.float32)
    sel = np.zeros((N_HEADS, KV_WIDTH, ATT_WIDTH), np.float32)
    for h in range(N_HEADS):
        g = h // (N_HEADS // KV_HEADS)
        for e in range(HEAD_DIM):
            pq[h * HEAD_DIM + e, h * MXU_DIM + g * HEAD_DIM + e] = 1.0
            sel[h, g * HEAD_DIM + e, h * HEAD_DIM + e] = 1.0
    bf = lambda a: a.astype(BF16)
    return dict(
        ffn1=(ffn1_norm[l], bf(ffn1_w_gate[l]), bf(ffn1_w_up[l]), bf(ffn1_w_down[l])),
        ffn2=(ffn2_norm[l], bf(ffn2_w_gate[l]), bf(ffn2_w_up[l]), bf(ffn2_w_down[l])),
        proj=dict(mix_norm=mix_norm[l].reshape(1, d), wq=bf(wq), wk=bf(wk), wv=bf(wv),
                  wqi=bf(wqi_arr.reshape(d, IDX_HEADS * MXU_DIM)), wki=bf(wki_arr), wwi=bf(wwi_arr),
                  wh=bf(wh), wg=bf(wg), bd=jnp.asarray(bd, BF16), pq=jnp.asarray(pq, BF16),
                  qn=jnp.tile(q_norm[l], N_HEADS).reshape(1, ATT_WIDTH),
                  kn=jnp.tile(k_norm[l], KV_HEADS).reshape(1, KV_WIDTH)),
        sel=jnp.asarray(sel, BF16),
        hg_norm=hg_norm[l],
        merge=(bf(w_branch_attn[l]), bf(w_branch_hgrn[l]), bf(w_out[l])),
    )


def kernel(x_prompt, x_sample, cache_k, cache_v, cache_idx_k, state_hgrn, page_table, hg_lower_bound,
           ffn1_norm, ffn1_w_gate, ffn1_w_up, ffn1_w_down, mix_norm, w_in, q_norm, k_norm, hg_norm,
           w_branch_attn, w_branch_hgrn, w_out, ffn2_norm, ffn2_w_gate, ffn2_w_up, ffn2_w_down):
    b, t, d = x_prompt.shape
    db, tn, _ = x_sample.shape
    depth = w_in.shape[0]
    n_pool, page = cache_k.shape[1], cache_k.shape[2]
    past = page_table.shape[1] * page
    cache_k2 = cache_k.reshape(depth, n_pool, page, KV_WIDTH)
    cache_v2 = cache_v.reshape(depth, n_pool, page, KV_WIDTH)
    lb_param = hg_lower_bound.astype(F32)
    tn_pad = -(-tn // HG_SUB) * HG_SUB

    xp = x_prompt.reshape(b * t, d)
    xs = x_sample.reshape(db * tn, d)
    outs = [[] for _ in range(8)]
    for l in range(depth):
        w = _prep_layer(l, ffn1_norm, ffn1_w_gate, ffn1_w_up, ffn1_w_down, mix_norm, w_in, q_norm, k_norm,
                        hg_norm, w_branch_attn, w_branch_hgrn, w_out, ffn2_norm, ffn2_w_gate, ffn2_w_up,
                        ffn2_w_down)
        xp = _ffn(xp, *w["ffn1"])
        qx, k, v, vb, qc, ki, wi, ph, pg, kt, kct = _proj(xp, b, t, 0, w["proj"], transposed=True)
        att = _dsa_prompt(qx.reshape(b, t, -1), qc.reshape(b, t, -1), wi.reshape(b, t, -1),
                          kct, kt, vb.reshape(b, t, -1), w["sel"])
        rec, s_p = _hgrn(ph.reshape(b, t, -1), lb_param, jnp.zeros((b, HG_HEADS, HG_KDIM, HG_VDIM), F32),
                         w["hg_norm"], l, t)
        xp = _merge(xp, att.reshape(b * t, -1), rec.reshape(b * t, -1), pg, *w["merge"])
        xp = _ffn(xp, *w["ffn2"])
        outs[0].append(k.reshape(b, t, KV_HEADS, HEAD_DIM))
        outs[1].append(v.reshape(b, t, KV_HEADS, HEAD_DIM))
        outs[2].append(ki.reshape(b, t, IDX_DIM))
        outs[3].append(s_p.astype(x_prompt.dtype))

        xs = _ffn(xs, *w["ffn1"])
        qx, k, v, vb, qc, ki, wi, ph, pg, kb, kc = _proj(xs, db, tn, past, w["proj"], transposed=False)
        rows = tn * N_HEADS
        qx3 = qx.reshape(db, rows, MXU_DIM)
        qc3 = qc.reshape(db, rows, MXU_DIM)
        wb3 = jnp.broadcast_to(wi[:, :IDX_HEADS].reshape(db, rows, 1), (db, rows, LANES))
        pad_new = lambda a: jnp.pad(a.reshape(db, tn, -1), ((0, 0), (0, LANES - tn), (0, 0)))
        o_s = _dsa_sample(page_table, qx3, qc3, wb3, pad_new(kc), pad_new(kb), pad_new(vb),
                          cache_idx_k, cache_k2, cache_v2, l)
        o_s = o_s.reshape(db, tn, KV_HEADS, N_HEADS // KV_HEADS, KV_HEADS, HEAD_DIM)
        att_s = jnp.stack([o_s[:, :, g, :, g, :] for g in range(KV_HEADS)], axis=2)
        att_s = att_s.reshape(db * tn, ATT_WIDTH).astype(BF16)
        ph3 = jnp.pad(ph.reshape(db, tn, -1), ((0, 0), (0, tn_pad - tn), (0, 0)))
        rec, s_s = _hgrn(ph3, lb_param, state_hgrn[l], w["hg_norm"], l, tn)
        rec = rec[:, :tn].reshape(db * tn, HG_WIDTH)
        xs = _merge(xs, att_s, rec, pg, *w["merge"])
        xs = _ffn(xs, *w["ffn2"])
        outs[4].append(k.reshape(db, tn, KV_HEADS, HEAD_DIM))
        outs[5].append(v.reshape(db, tn, KV_HEADS, HEAD_DIM))
        outs[6].append(ki.reshape(db, tn, IDX_DIM))
        outs[7].append(s_s.astype(state_hgrn.dtype))

    return (xp.reshape(b, t, d), xs.reshape(db, tn, d)) + tuple(jnp.stack(o) for o in outs)
```

```python
import functools

import numpy as np
import jax
import jax.numpy as jnp
from jax import lax
from jax.experimental import pallas as pl
from jax.experimental.pallas import tpu as pltpu

F32, BF16, I32 = jnp.float32, jnp.bfloat16, jnp.int32

N_HEADS, KV_HEADS, HEAD_DIM = 8, 4, 64
IDX_HEADS, IDX_DIM, TOPK_MAX = 8, 64, 256
HG_HEADS, HG_KDIM, HG_VDIM = 4, 128, 128
ATT_WIDTH = N_HEADS * HEAD_DIM
KV_WIDTH = KV_HEADS * HEAD_DIM
HG_WIDTH = HG_HEADS * HG_KDIM
ROPE_THETA = 10000.0
NORM_EPS = 1e-6

LANES = 128
MXU_DIM = 256
VMEM_LIMIT_BYTES = 56 * 2**20
INT_MIN = -2**31
MASKED_MAX = -1e30
LOG2E = 1.4426950408889634

ROW_TILE = 512
PROJ_ROW_TILE = 256
FFN_CHUNK = 256
HG_CHUNK = 64
HG_SUB = 16
Q_TILE = 256
K_CHUNK = 512
ATT_ROWS = 128
COUNT_ROWS = 64
SAMPLE_CHUNK = 2048


def _params(n_axes):
    return pltpu.CompilerParams(dimension_semantics=("arbitrary",) * n_axes,
                                vmem_limit_bytes=VMEM_LIMIT_BYTES)


def _resident(shape, n_axes):
    zeros = (0,) * len(shape)
    if n_axes == 1:
        return pl.BlockSpec(shape, lambda i: zeros, pipeline_mode=pl.Buffered(1))
    return pl.BlockSpec(shape, lambda i, j: zeros, pipeline_mode=pl.Buffered(1))


def _rms(x, gain):
    return x * lax.rsqrt(jnp.mean(x * x, axis=-1, keepdims=True) + NORM_EPS) * gain


def _sigmoid(x):
    return 1.0 / (1.0 + jnp.exp(-x))


def _dot(a, b):
    return jnp.dot(a, b, preferred_element_type=F32)


def _dot_nt(a, b):
    return lax.dot_general(a, b, (((1,), (1,)), ((), ())), preferred_element_type=F32)


def _dot_tn(a, b):
    return lax.dot_general(a, b, (((0,), (0,)), ((), ())), preferred_element_type=F32)


def _split2(x):
    hi = x.astype(BF16)
    return hi, (x - hi.astype(F32)).astype(BF16)


def _ffn_kernel(x_ref, g_ref, wg_ref, wu_ref, wd_ref, o_ref):
    x = x_ref[...]
    h = _rms(x, g_ref[...]).astype(BF16)
    d_ff = wg_ref.shape[1]
    acc = jnp.zeros(x.shape, F32)
    for c in range(d_ff // FFN_CHUNK):
        cols = slice(c * FFN_CHUNK, (c + 1) * FFN_CHUNK)
        g = _dot(h, wg_ref[:, cols])
        u = _dot(h, wu_ref[:, cols])
        a = (g * _sigmoid(g) * u).astype(BF16)
        acc = acc + _dot(a, wd_ref[cols, :])
    o_ref[...] = x + 0.5 * acc


def _ffn(x, norm, wg, wu, wd):
    n, d = x.shape
    tm = min(ROW_TILE, n)
    assert n % tm == 0 and wg.shape[1] % FFN_CHUNK == 0
    return pl.pallas_call(
        _ffn_kernel,
        grid=(n // tm,),
        in_specs=[pl.BlockSpec((tm, d), lambda i: (i, 0)), _resident((1, d), 1),
                  _resident(wg.shape, 1), _resident(wu.shape, 1), _resident(wd.shape, 1)],
        out_specs=pl.BlockSpec((tm, d), lambda i: (i, 0)),
        out_shape=jax.ShapeDtypeStruct((n, d), F32),
        compiler_params=_params(1),
        name="ffn",
    )(x, norm.reshape(1, d), wg, wu, wd)


def _rope(y, cos, sin):
    n = y.shape[1]
    reps = n // LANES
    c = jnp.tile(cos, (1, reps)) if reps > 1 else cos
    s = jnp.tile(sin, (1, reps)) if reps > 1 else sin
    lane = lax.broadcasted_iota(I32, y.shape, 1)
    first_half = (lane & (HEAD_DIM // 2)) == 0
    partner = jnp.where(first_half, pltpu.roll(y, n - HEAD_DIM // 2, axis=1),
                        pltpu.roll(y, HEAD_DIM // 2, axis=1))
    return y * c + partner * s


def _proj_kernel(x_ref, gn_ref, wq_ref, wk_ref, wv_ref, wqi_ref, wki_ref, wwi_ref, wh_ref, wg_ref,
                 bd_ref, pq_ref, qn_ref, kn_ref, cos_ref, sin_ref,
                 qx_ref, k_ref, v_ref, vb_ref, qc_ref, ki_ref, wi_ref, ph_ref, pg_ref, kb_ref, kc_ref,
                 *, transposed):
    h = _rms(x_ref[...], gn_ref[...]).astype(BF16)
    cos, sin = cos_ref[...], sin_ref[...]

    def head_norm(y, gain):
        n = y.shape[1]
        hi, lo = _split2(y * y)
        bd = bd_ref[:n, :n]
        ss = _dot(hi, bd) + _dot(lo, bd)
        return y * lax.rsqrt(ss * (1.0 / HEAD_DIM) + NORM_EPS) * gain

    q = _rope(head_norm(_dot(h, wq_ref[...]), qn_ref[...]), cos, sin) * (LOG2E * HEAD_DIM ** -0.5)
    qx_ref[...] = _dot(q.astype(BF16), pq_ref[...]).astype(BF16)

    k = _rope(head_norm(_dot(h, wk_ref[...]), kn_ref[...]), cos, sin)
    k_ref[...] = k
    v = _dot(h, wv_ref[...])
    v_ref[...] = v
    vb_ref[...] = v.astype(BF16)

    qi = _rope(_dot(h, wqi_ref[...]), cos, sin) * (IDX_DIM ** -0.5)
    lane = lax.broadcasted_iota(I32, qi.shape, 1) & (MXU_DIM - 1)
    is_lo = (lane >= IDX_DIM) & (lane < 2 * IDX_DIM)
    qc_ref[...] = jnp.where(is_lo, qi - qi.astype(BF16).astype(F32), qi).astype(BF16)

    ki = _rope(_dot(h, wki_ref[...]), cos, sin)
    ki_ref[...] = ki[:, :IDX_DIM]
    lane = lax.broadcasted_iota(I32, ki.shape, 1)
    is_lo = (lane >= 2 * IDX_DIM) & (lane < 3 * IDX_DIM)
    kc = jnp.where(is_lo, ki - ki.astype(BF16).astype(F32), ki)

    if transposed:
        kb_ref[0] = k.T.astype(BF16)
        kc_ref[0] = kc.T.astype(BF16)
    else:
        kb_ref[...] = k.astype(BF16)
        kc_ref[...] = kc.astype(BF16)

    wi_ref[...] = _dot(h, wwi_ref[...]) * (IDX_HEADS ** -0.5)
    ph_ref[...] = _dot(h, wh_ref[...])
    pg_ref[...] = _dot(h, wg_ref[...])


def _proj(x, batch, seq, pos0, w, transposed):
    n, d = x.shape
    tm = min(PROJ_ROW_TILE, n)
    assert n % tm == 0
    lane = np.arange(LANES)
    inv = ROPE_THETA ** (-(lane % (HEAD_DIM // 2)).astype(np.float32) * 2.0 / HEAD_DIM)
    sign = np.where((lane % HEAD_DIM) < HEAD_DIM // 2, -1.0, 1.0).astype(np.float32)
    ang = (pos0 + jnp.arange(seq)).astype(F32)[:, None] * jnp.asarray(inv)[None, :]
    cos_t, sin_t = jnp.cos(ang), jnp.sin(ang) * jnp.asarray(sign)[None, :]
    if seq >= tm:
        assert seq % tm == 0
        per_seq = seq // tm
        tab_map = lambda i: (i % per_seq, 0)
    else:
        assert tm % seq == 0
        cos_t, sin_t = jnp.tile(cos_t, (tm // seq, 1)), jnp.tile(sin_t, (tm // seq, 1))
        tab_map = lambda i: (0, 0)

    row = lambda width: pl.BlockSpec((tm, width), lambda i: (i, 0))
    if transposed:
        per_seq = seq // tm
        kt_spec = pl.BlockSpec((1, MXU_DIM, tm), lambda i: (i // per_seq, 0, i % per_seq))
        kt_shape = jax.ShapeDtypeStruct((batch, MXU_DIM, seq), BF16)
    else:
        kt_spec = row(MXU_DIM)
        kt_shape = jax.ShapeDtypeStruct((n, MXU_DIM), BF16)

    weights = [w["wq"], w["wk"], w["wv"], w["wqi"], w["wki"], w["wwi"], w["wh"], w["wg"],
               w["bd"], w["pq"], w["qn"], w["kn"]]
    out_shape = [
        jax.ShapeDtypeStruct((n, N_HEADS * MXU_DIM), BF16),
        jax.ShapeDtypeStruct((n, KV_WIDTH), F32),
        jax.ShapeDtypeStruct((n, KV_WIDTH), F32),
        jax.ShapeDtypeStruct((n, KV_WIDTH), BF16),
        jax.ShapeDtypeStruct((n, IDX_HEADS * MXU_DIM), BF16),
        jax.ShapeDtypeStruct((n, IDX_DIM), F32),
        jax.ShapeDtypeStruct((n, LANES), F32),
        jax.ShapeDtypeStruct((n, 4 * HG_WIDTH), F32),
        jax.ShapeDtypeStruct((n, 2 * d), F32),
        kt_shape,
        kt_shape,
    ]
    out_specs = [row(N_HEADS * MXU_DIM), row(KV_WIDTH), row(KV_WIDTH), row(KV_WIDTH),
                 row(IDX_HEADS * MXU_DIM), row(IDX_DIM), row(LANES), row(4 * HG_WIDTH), row(2 * d),
                 kt_spec, kt_spec]
    return pl.pallas_call(
        functools.partial(_proj_kernel, transposed=transposed),
        grid=(n // tm,),
        in_specs=[row(d), _resident((1, d), 1)] + [_resident(a.shape, 1) for a in weights]
                 + [pl.BlockSpec((tm, LANES), tab_map), pl.BlockSpec((tm, LANES), tab_map)],
        out_specs=out_specs,
        out_shape=out_shape,
        compiler_params=_params(1),
        name="proj",
    )(x, w["mix_norm"], *weights, cos_t, sin_t)


def _hgrn_kernel(ph_ref, lbp_ref, s0_ref, gn_ref, tri_ref, rec_ref, s_ref, st_ref, o_ref,
                 *, layer, chunk, sub, t_valid):
    j = pl.program_id(1)

    @pl.when(j == 0)
    def _():
        for hh in range(HG_HEADS):
            st_ref[hh] = s0_ref[0, hh].T

    lbp = lbp_ref[...]
    e = jnp.exp(lbp - jnp.max(lbp, axis=0, keepdims=True))
    lb_all = jnp.sum(e[:layer + 1], axis=0, keepdims=True) / jnp.sum(e, axis=0, keepdims=True)

    tri = tri_ref[...]
    row_c = lax.broadcasted_iota(I32, (chunk, HG_KDIM), 0)
    row_s = lax.broadcasted_iota(I32, (sub, HG_KDIM), 0)
    valid = (j * chunk + row_c) < t_valid
    w = HG_WIDTH
    for hh in range(HG_HEADS):
        lanes = slice(hh * HG_KDIM, (hh + 1) * HG_KDIM)
        hq = ph_ref[0, :, hh * HG_KDIM:(hh + 1) * HG_KDIM]
        z = ph_ref[0, :, w + hh * HG_KDIM:w + (hh + 1) * HG_KDIM]
        v = ph_ref[0, :, 2 * w + hh * HG_VDIM:2 * w + (hh + 1) * HG_VDIM]
        hg = ph_ref[0, :, 3 * w + hh * HG_VDIM:3 * w + (hh + 1) * HG_VDIM]
        lb = lb_all[:, lanes]
        q = hq * _sigmoid(hq) * (HG_KDIM ** -0.5)
        logf = jnp.where(valid, jnp.log(lb + (1.0 - lb) * _sigmoid(z)), 0.0)
        kk = jnp.where(valid, (1.0 - lb) * _sigmoid(-z), 0.0)

        l1 = logf.astype(BF16)
        r1 = logf - l1.astype(F32)
        l2 = r1.astype(BF16)
        l3 = (r1 - l2.astype(F32)).astype(BF16)
        g = _dot(tri, l1) + _dot(tri, l2) + _dot(tri, l3)
        g_last = g[chunk - 1:chunk, :]

        st = st_ref[hh]
        vb = v.astype(BF16)
        o_ref[hh] = _dot_nt((q * jnp.exp(g)).astype(BF16), st.astype(BF16))
        for i in range(chunk // sub):
            r0 = i * sub
            gi, qi, ki_, vi = g[r0:r0 + sub], q[r0:r0 + sub], kk[r0:r0 + sub], v[r0:r0 + sub]
            if i > 0:
                ref = g[r0 - 1:r0, :]
                q_rel = (qi * jnp.exp(gi - ref)).astype(BF16)
                k_rel = (kk[:r0] * jnp.exp(ref - g[:r0])).astype(BF16)
                a = _dot_nt(q_rel, k_rel)
                o_ref[hh, r0:r0 + sub, :] += _dot(a.astype(BF16), vb[:r0])
            for t in range(sub):
                dec = jnp.exp(jnp.where(row_s <= t, gi[t:t + 1, :] - gi, -jnp.inf))
                a_t = jnp.sum(qi[t:t + 1, :] * dec * ki_, axis=1, keepdims=True)
                o_ref[hh, r0 + t:r0 + t + 1, :] += jnp.sum(a_t * vi, axis=0, keepdims=True)

        k_end = (kk * jnp.exp(g_last - g)).astype(BF16)
        st_ref[hh] = jnp.exp(g_last) * st + _dot_tn(vb, k_end)

        o = o_ref[hh]
        gate = hg * _sigmoid(hg)
        rec_ref[0, :, hh * HG_VDIM:(hh + 1) * HG_VDIM] = (_rms(o, gn_ref[...]) * gate).astype(BF16)

    @pl.when(j == pl.num_programs(1) - 1)
    def _():
        for hh in range(HG_HEADS):
            s_ref[0, hh] = st_ref[hh].T


def _hgrn(ph, lb_param, s0, hg_norm, layer, t_valid):
    b, t, _ = ph.shape
    chunk = min(HG_CHUNK, t)
    sub = min(HG_SUB, chunk)
    assert t % chunk == 0 and chunk % sub == 0
    tri = jnp.asarray(np.tril(np.ones((chunk, chunk), np.float32)), BF16)
    return pl.pallas_call(
        functools.partial(_hgrn_kernel, layer=layer, chunk=chunk, sub=sub, t_valid=t_valid),
        grid=(b, t // chunk),
        in_specs=[pl.BlockSpec((1, chunk, 4 * HG_WIDTH), lambda i, j: (i, j, 0)),
                  _resident(lb_param.shape, 2),
                  pl.BlockSpec((1, HG_HEADS, HG_KDIM, HG_VDIM), lambda i, j: (i, 0, 0, 0)),
                  _resident((1, HG_VDIM), 2), _resident(tri.shape, 2)],
        out_specs=[pl.BlockSpec((1, chunk, HG_WIDTH), lambda i, j: (i, j, 0)),
                   pl.BlockSpec((1, HG_HEADS, HG_KDIM, HG_VDIM), lambda i, j: (i, 0, 0, 0))],
        out_shape=[jax.ShapeDtypeStruct((b, t, HG_WIDTH), BF16),
                   jax.ShapeDtypeStruct((b, HG_HEADS, HG_KDIM, HG_VDIM), F32)],
        scratch_shapes=[pltpu.VMEM((HG_HEADS, HG_VDIM, HG_KDIM), F32),
                        pltpu.VMEM((HG_HEADS, chunk, HG_VDIM), F32)],
        compiler_params=_params(2),
        name="hgrn",
    )(ph, lb_param, s0, hg_norm.reshape(1, HG_VDIM), tri)


def _sortable_key(score):
    bits = pltpu.bitcast(score, I32)
    return jnp.where(bits < 0, INT_MIN - bits, bits)


def _kth_largest(count_ge, topk, shape):
    zero = jnp.zeros(shape, I32)
    c = count_ge(zero)
    ok = c >= topk
    tau = jnp.where(ok, zero, jnp.full(shape, INT_MIN, I32))
    n_ge = jnp.where(ok, c, 0)

    def step(it, carry):
        tau, n_ge = carry
        cand = tau | lax.shift_left(jnp.int32(1), 30 - it)
        c = count_ge(cand)
        ok = c >= topk
        return jnp.where(ok, cand, tau), jnp.where(ok, c, n_ge)

    return lax.fori_loop(0, 31, step, (tau, n_ge))


def _dsa_prompt_kernel(qx_ref, qc_ref, wi_ref, kct_ref, kt_ref, v_ref, tri_ref, sel_ref, o_ref,
                       keys_ref, aq_ref, ac_ref, wb_ref, m_ref, l_ref, acc_ref, lg_ref, p_ref, bias_ref, *, tq, ck, topk):
    i = pl.program_id(1)
    row0 = i * tq
    n_chunks = (row0 + tq + ck - 1) // ck
    n_sub = ck // LANES
    rows_q = N_HEADS * tq

    for h in range(IDX_HEADS):
        ac_ref[h * tq:(h + 1) * tq, :] = qc_ref[0, :, h * MXU_DIM:(h + 1) * MXU_DIM]
        aq_ref[h * tq:(h + 1) * tq, :] = qx_ref[0, :, h * MXU_DIM:(h + 1) * MXU_DIM]
        wb_ref[h] = jnp.broadcast_to(wi_ref[0, :, h:h + 1], (tq, LANES))

    row_id = row0 + lax.broadcasted_iota(I32, (tq, ck), 0)
    col_id = lax.broadcasted_iota(I32, (tq, ck), 1)

    def score_body(c, carry):
        col0 = pl.multiple_of(c * ck, ck)
        s = _dot(ac_ref[...], kct_ref[0, :, pl.ds(col0, ck)])
        tot = jnp.zeros((tq, ck), F32)
        for h in range(IDX_HEADS):
            tot = tot + jnp.maximum(s[h * tq:(h + 1) * tq, :], 0.0) * jnp.tile(wb_ref[h], (1, n_sub))
        keys_ref[:, pl.ds(col0, ck)] = jnp.where(col0 + col_id <= row_id, _sortable_key(tot), INT_MIN)
        return carry

    lax.fori_loop(0, n_chunks, score_body, 0)

    def count(cmp, cand):
        out = []
        for rb in range(tq // COUNT_ROWS):
            rows = slice(rb * COUNT_ROWS, (rb + 1) * COUNT_ROWS)
            cand_b = cand[rows]

            def body(c, cnt):
                col0 = pl.multiple_of(c * ck, ck)
                blk = keys_ref[rows, pl.ds(col0, ck)]
                for u in range(n_sub):
                    cnt = cnt + jnp.where(cmp(blk[:, u * LANES:(u + 1) * LANES], cand_b), 1, 0)
                return cnt
            out.append(lax.fori_loop(0, n_chunks, body, jnp.zeros((COUNT_ROWS, LANES), I32)))
        return jnp.sum(jnp.concatenate(out, axis=0), axis=1, keepdims=True)

    tau, n_ge = _kth_largest(lambda cand: count(lambda a, b: a >= b, cand), topk, (tq, LANES))
    tau_sel = jnp.maximum(tau, INT_MIN + 1)
    has_ties = jnp.max(n_ge) > topk

    m_ref[...] = jnp.full(m_ref.shape, MASKED_MAX, F32)
    l_ref[...] = jnp.zeros(l_ref.shape, F32)
    acc_ref[...] = jnp.zeros(acc_ref.shape, F32)

    def attend(ranked):
        tau_t = jnp.tile(tau_sel, (1, n_sub))
        if ranked:
            need_eq = (topk - count(lambda a, b: a > b, tau_sel)).astype(F32)

        def body(c, eq_seen):
            col0 = pl.multiple_of(c * ck, ck)
            blk = keys_ref[:, pl.ds(col0, ck)]
            if ranked:
                eq = blk == tau_t
                eq_f = jnp.where(eq, 1.0, 0.0)
                before = _dot(eq_f.astype(BF16), tri_ref[...]) + eq_seen
                rank = jnp.where(eq, before, jnp.where(blk > tau_t, -1.0, jnp.inf))
                bias_ref[...] = jnp.where(rank < need_eq, 0.0, -jnp.inf)
                eq_seen = eq_seen + jnp.sum(eq_f, axis=1, keepdims=True)
            else:
                bias_ref[...] = jnp.where(blk >= tau_t, 0.0, -jnp.inf)
            for g in range(rows_q // ATT_ROWS):
                rows = slice(g * ATT_ROWS, (g + 1) * ATT_ROWS)
                off = (g * ATT_ROWS) % tq
                lg_ref[rows, :] = (_dot(aq_ref[rows, :], kt_ref[0, :, pl.ds(col0, ck)])
                                   + bias_ref[off:off + ATT_ROWS, :])
                m_old = m_ref[rows, :]
                m_new = jnp.maximum(m_old, jnp.max(lg_ref[rows, :], axis=1, keepdims=True))
                m_ref[rows, :] = m_new
                alpha = jnp.exp2(m_old - m_new)
                p = jnp.exp2(lg_ref[rows, :] - jnp.tile(m_new, (1, n_sub)))
                p_ref[rows, :] = p.astype(BF16)
                l_ref[rows, :] = alpha * l_ref[rows, :] + jnp.sum(p, axis=1, keepdims=True)
                acc_ref[rows, :] = (jnp.tile(alpha, (1, KV_WIDTH // LANES)) * acc_ref[rows, :]
                                    + _dot(p_ref[rows, :], v_ref[0, pl.ds(col0, ck), :]))
            return eq_seen
        lax.fori_loop(0, n_chunks, body, jnp.zeros((tq, 1), F32))

    @pl.when(has_ties)
    def _():
        attend(True)

    @pl.when(jnp.logical_not(has_ties))
    def _():
        attend(False)

    out = jnp.zeros((tq, ATT_WIDTH), F32)
    for h in range(N_HEADS):
        rows = slice(h * tq, (h + 1) * tq)
        o_h = acc_ref[rows, :] / jnp.tile(l_ref[rows, :], (1, KV_WIDTH // LANES))
        out = out + _dot(o_h.astype(BF16), sel_ref[h])
    o_ref[0] = out.astype(BF16)


def _dsa_prompt(qx, qc, wi, kct, kt, vb, sel):
    b, t, _ = qx.shape
    tq, ck = min(Q_TILE, t), min(K_CHUNK, t)
    assert t % tq == 0 and t % ck == 0 and ck % tq == 0
    topk = min(TOPK_MAX, t // 4)
    tri = jnp.asarray(np.triu(np.ones((ck, ck), np.float32), 1), BF16)
    tile = lambda width: pl.BlockSpec((1, tq, width), lambda i, j: (i, j, 0))
    whole = lambda rows, cols: pl.BlockSpec((1, rows, cols), lambda i, j: (i, 0, 0),
                                            pipeline_mode=pl.Buffered(1))
    return pl.pallas_call(
        functools.partial(_dsa_prompt_kernel, tq=tq, ck=ck, topk=topk),
        grid=(b, t // tq),
        in_specs=[tile(N_HEADS * MXU_DIM), tile(IDX_HEADS * MXU_DIM), tile(LANES),
                  whole(MXU_DIM, t), whole(MXU_DIM, t), whole(t, KV_WIDTH),
                  _resident(tri.shape, 2), _resident(sel.shape, 2)],
        out_specs=tile(ATT_WIDTH),
        out_shape=jax.ShapeDtypeStruct((b, t, ATT_WIDTH), BF16),
        scratch_shapes=[pltpu.VMEM((tq, t), I32),
                        pltpu.VMEM((N_HEADS * tq, MXU_DIM), BF16),
                        pltpu.VMEM((IDX_HEADS * tq, MXU_DIM), BF16),
                        pltpu.VMEM((IDX_HEADS, tq, LANES), F32),
                        pltpu.VMEM((N_HEADS * tq, LANES), F32),
                        pltpu.VMEM((N_HEADS * tq, LANES), F32),
                        pltpu.VMEM((N_HEADS * tq, KV_WIDTH), F32),
                        pltpu.VMEM((N_HEADS * tq, ck), F32),
                        pltpu.VMEM((N_HEADS * tq, ck), BF16),
                        pltpu.VMEM((tq, ck), F32)],
        compiler_params=_params(2),
        name="dsa_prompt",
    )(qx, qc, wi, kct, kt, vb, tri, sel)


def _dsa_sample_kernel(pt_ref, qx_ref, qc_ref, wb_ref, kcn_ref, kn_ref, vn_ref, tri_ref,
                       cik_hbm, ck_hbm, cv_hbm, o_ref,
                       ik_buf, k_buf, v_buf, keys_ref, bias_ref, lg_ref, sem,
                       *, layer, n_pages, page, chunk, tn, topk):
    b = pl.program_id(0)
    nb = pl.num_programs(0)
    past = n_pages * page
    width = past + LANES
    rows = tn * N_HEADS

    def copies(sample, slot, p):
        pg = pt_ref[sample, p]
        dst = pl.ds(pl.multiple_of(p * page, page), page)
        return (pltpu.make_async_copy(cik_hbm.at[layer, pg], ik_buf.at[slot, :, dst], sem.at[slot, 0]),
                pltpu.make_async_copy(ck_hbm.at[layer, pg], k_buf.at[slot, :, dst], sem.at[slot, 1]),
                pltpu.make_async_copy(cv_hbm.at[layer, pg], v_buf.at[slot, :, dst], sem.at[slot, 2]))

    def fetch(sample, slot):
        def body(p, carry):
            for cp in copies(sample, slot, p):
                cp.start()
            return carry
        lax.fori_loop(0, n_pages, body, 0)

    def wait(sample, slot):
        def body(p, carry):
            for cp in copies(sample, slot, p):
                cp.wait()
            return carry
        lax.fori_loop(0, n_pages, body, 0)

    slot = b % 2

    @pl.when(b == 0)
    def _():
        fetch(0, 0)

    @pl.when(b + 1 < nb)
    def _():
        fetch(b + 1, 1 - slot)

    wait(b, slot)

    qc = qc_ref[0]
    qx = qx_ref[0]
    wb = wb_ref[0]
    tok = lax.broadcasted_iota(I32, (tn, LANES), 0)
    col = lax.broadcasted_iota(I32, (tn, LANES), 1)

    def head_sum(s):
        n = s.shape[1]
        r = jnp.maximum(s, 0.0) * jnp.tile(wb, (1, n // LANES))
        return jnp.sum(r.reshape(tn, N_HEADS, n), axis=1)

    keys_ref[...] = jnp.full(keys_ref.shape, INT_MIN, I32)

    zero_rows = jnp.zeros((MXU_DIM - 3 * IDX_DIM, chunk), BF16)
    for c in range(past // chunk):
        cols = slice(c * chunk, (c + 1) * chunk)
        hi, lo = _split2(ik_buf[slot, :, cols])
        kct = jnp.concatenate([hi, hi, lo, zero_rows], axis=0)
        keys_ref[0:tn, cols] = _sortable_key(head_sum(_dot(qc, kct)))
    s_new = head_sum(_dot(qc, kcn_ref[0]))
    keys_ref[0:tn, past:width] = jnp.where(col <= tok, _sortable_key(s_new), INT_MIN)

    def count_ge(cand):
        return jnp.sum(jnp.where(keys_ref[...] >= cand, 1, 0), axis=1, keepdims=True)

    tau, n_ge = _kth_largest(count_ge, topk, (8, 1))
    tau_sel = jnp.maximum(tau, INT_MIN + 1)
    has_ties = jnp.max(n_ge) > topk

    @pl.when(jnp.logical_not(has_ties))
    def _():
        bias_ref[...] = jnp.where(keys_ref[...] >= tau_sel, 0.0, -jnp.inf)

    @pl.when(has_ties)
    def _():
        n_gt = jnp.sum(jnp.where(keys_ref[...] > tau_sel, 1, 0), axis=1, keepdims=True)
        need_eq = (topk - n_gt).astype(F32)

        def body(u, eq_seen):
            c0 = pl.multiple_of(u * LANES, LANES)
            blk = keys_ref[:, pl.ds(c0, LANES)]
            eq = blk == tau_sel
            eq_f = jnp.where(eq, 1.0, 0.0)
            before = _dot(eq_f.astype(BF16), tri_ref[...]) + eq_seen
            rank = jnp.where(eq, before, jnp.where(blk > tau_sel, -1.0, jnp.inf))
            bias_ref[:, pl.ds(c0, LANES)] = jnp.where(rank < need_eq, 0.0, -jnp.inf)
            return eq_seen + jnp.sum(eq_f, axis=1, keepdims=True)

        lax.fori_loop(0, width // LANES, body, jnp.zeros((8, 1), F32))

    def expand(x):
        n = x.shape[1]
        return jnp.broadcast_to(x[0:tn].reshape(tn, 1, n), (tn, N_HEADS, n)).reshape(rows, n)

    for c in range(past // chunk):
        cols = slice(c * chunk, (c + 1) * chunk)
        lg_ref[:, cols] = _dot(qx, k_buf[slot, :, cols].astype(BF16)) + expand(bias_ref[:, cols])
    lg_ref[:, past:width] = _dot(qx, kn_ref[0]) + expand(bias_ref[:, past:width])
    lg = lg_ref[...]
    p = jnp.exp2(lg - jnp.max(lg, axis=1, keepdims=True))
    l = jnp.sum(p, axis=1, keepdims=True)
    lg_ref[...] = p
    acc = _dot_nt(lg_ref[:, past:width].astype(BF16), vn_ref[0])
    for c in range(past // chunk):
        cols = slice(c * chunk, (c + 1) * chunk)
        acc = acc + _dot_nt(lg_ref[:, cols].astype(BF16), v_buf[slot, :, cols].astype(BF16))
    o_ref[0] = acc / l


def _dsa_sample(page_table, qx, qc, wb, kcn, kn, vn, cache_ik, cache_k, cache_v, layer):
    db, rows, _ = qx.shape
    tn = rows // N_HEADS
    n_pages = page_table.shape[1]
    page = cache_ik.shape[3]
    past = n_pages * page
    chunk = min(SAMPLE_CHUNK, past)
    assert page == LANES and tn <= 8 and past % chunk == 0
    tri = jnp.asarray(np.triu(np.ones((page, page), np.float32), 1), BF16)
    topk = min(TOPK_MAX, (past + tn) // 4)
    per = lambda r, c: pl.BlockSpec((1, r, c), lambda i, pt: (i, 0, 0))
    grid_spec = pltpu.PrefetchScalarGridSpec(
        num_scalar_prefetch=1,
        grid=(db,),
        in_specs=[per(rows, MXU_DIM), per(rows, MXU_DIM), per(rows, LANES),
                  per(MXU_DIM, LANES), per(KV_WIDTH, LANES), per(KV_WIDTH, LANES),
                  pl.BlockSpec(tri.shape, lambda i, pt: (0, 0)),
                  pl.BlockSpec(memory_space=pl.ANY), pl.BlockSpec(memory_space=pl.ANY),
                  pl.BlockSpec(memory_space=pl.ANY)],
        out_specs=per(rows, KV_WIDTH),
        scratch_shapes=[pltpu.VMEM((2, IDX_DIM, past), F32),
                        pltpu.VMEM((2, KV_WIDTH, past), F32),
                        pltpu.VMEM((2, KV_WIDTH, past), F32),
                        pltpu.VMEM((8, past + LANES), I32),
                        pltpu.VMEM((8, past + LANES), F32),
                        pltpu.VMEM((rows, past + LANES), F32),
                        pltpu.SemaphoreType.DMA((2, 3))],
    )
    return pl.pallas_call(
        functools.partial(_dsa_sample_kernel, layer=layer, n_pages=n_pages, page=page, chunk=chunk,
                          tn=tn, topk=topk),
        grid_spec=grid_spec,
        out_shape=jax.ShapeDtypeStruct((db, rows, KV_WIDTH), F32),
        compiler_params=_params(1),
        name="dsa_sample",
    )(page_table, qx, qc, wb, kcn, kn, vn, tri, cache_ik, cache_k, cache_v)


def _merge_kernel(x_ref, att_ref, rec_ref, pg_ref, wa_ref, wh_ref, wo_ref, o_ref):
    d = x_ref.shape[1]
    pg = pg_ref[...]
    merged = (_sigmoid(pg[:, :d]) * _dot(att_ref[...], wa_ref[...])
              + _sigmoid(pg[:, d:]) * _dot(rec_ref[...], wh_ref[...]))
    o_ref[...] = x_ref[...] + _dot(merged.astype(BF16), wo_ref[...])


def _merge(x, att, rec, pg, wa, wh, wo):
    n, d = x.shape
    tm = min(ROW_TILE, n)
    row = lambda width: pl.BlockSpec((tm, width), lambda i: (i, 0))
    return pl.pallas_call(
        _merge_kernel,
        grid=(n // tm,),
        in_specs=[row(d), row(ATT_WIDTH), row(HG_WIDTH), row(2 * d),
                  _resident(wa.shape, 1), _resident(wh.shape, 1), _resident(wo.shape, 1)],
        out_specs=row(d),
        out_shape=jax.ShapeDtypeStruct((n, d), F32),
        compiler_params=_params(1),
        name="merge",
    )(x, att, rec, pg, wa, wh, wo)


def _prep_layer(l, ffn1_norm, ffn1_w_gate, ffn1_w_up, ffn1_w_down, mix_norm, w_in, q_norm, k_norm, hg_norm,
                w_branch_attn, w_branch_hgrn, w_out, ffn2_norm, ffn2_w_gate, ffn2_w_up, ffn2_w_down):
    d = w_in.shape[1]
    w = w_in[l]
    off = [0]

    def take(width):
        a = w[:, off[0]:off[0] + width]
        off[0] += width
        return a

    wq, wk, wv = take(ATT_WIDTH), take(KV_WIDTH), take(KV_WIDTH)
    wqi, wki, wwi = take(IDX_HEADS * IDX_DIM), take(IDX_DIM), take(IDX_HEADS)
    wh = take(4 * HG_WIDTH)
    wg = take(2 * d)
    zeros = jnp.zeros((d, IDX_DIM), F32)
    wqi_heads = wqi.reshape(d, IDX_HEADS, IDX_DIM)
    wqi_arr = jnp.concatenate([wqi_heads, wqi_heads, wqi_heads, jnp.zeros_like(wqi_heads)], axis=2)
    wki_arr = jnp.concatenate([wki, wki, wki, zeros], axis=1)
    wwi_arr = jnp.concatenate([wwi, jnp.zeros((d, LANES - IDX_HEADS), F32)], axis=1)
    bd = np.kron(np.eye(ATT_WIDTH // HEAD_DIM, dtype=np.float32), np.ones((HEAD_DIM, HEAD_DIM), np.float32))
    pq = np.zeros((ATT_WIDTH, N_HEADS * MXU_DIM), np.float32)
    sel = np.zeros((N_HEADS, KV_WIDTH, ATT_WIDTH), np.float32)
    for h in range(N_HEADS):
        g = h // (N_HEADS // KV_HEADS)
        for e in range(HEAD_DIM):
            pq[h * HEAD_DIM + e, h * MXU_DIM + g * HEAD_DIM + e] = 1.0
            sel[h, g * HEAD_DIM + e, h * HEAD_DIM + e] = 1.0
    bf = lambda a: a.astype(BF16)
    return dict(
        ffn1=(ffn1_norm[l], bf(ffn1_w_gate[l]), bf(ffn1_w_up[l]), bf(ffn1_w_down[l])),
        ffn2=(ffn2_norm[l], bf(ffn2_w_gate[l]), bf(ffn2_w_up[l]), bf(ffn2_w_down[l])),
        proj=dict(mix_norm=mix_norm[l].reshape(1, d), wq=bf(wq), wk=bf(wk), wv=bf(wv),
                  wqi=bf(wqi_arr.reshape(d, IDX_HEADS * MXU_DIM)), wki=bf(wki_arr), wwi=bf(wwi_arr),
                  wh=bf(wh), wg=bf(wg), bd=jnp.asarray(bd, BF16), pq=jnp.asarray(pq, BF16),
                  qn=jnp.tile(q_norm[l], N_HEADS).reshape(1, ATT_WIDTH),
                  kn=jnp.tile(k_norm[l], KV_HEADS).reshape(1, KV_WIDTH)),
        sel=jnp.asarray(sel, BF16),
        hg_norm=hg_norm[l],
        merge=(bf(w_branch_attn[l]), bf(w_branch_hgrn[l]), bf(w_out[l])),
    )


def kernel(x_prompt, x_sample, cache_k, cache_v, cache_idx_k, state_hgrn, page_table, hg_lower_bound,
           ffn1_norm, ffn1_w_gate, ffn1_w_up, ffn1_w_down, mix_norm, w_in, q_norm, k_norm, hg_norm,
           w_branch_attn, w_branch_hgrn, w_out, ffn2_norm, ffn2_w_gate, ffn2_w_up, ffn2_w_down):
    b, t, d = x_prompt.shape
    db, tn, _ = x_sample.shape
    depth = w_in.shape[0]
    n_pool, page = cache_k.shape[1], cache_k.shape[2]
    past = page_table.shape[1] * page
    cache_ikt = cache_idx_k.transpose(0, 1, 3, 2)
    cache_kt = cache_k.transpose(0, 1, 3, 4, 2).reshape(depth, n_pool, KV_WIDTH, page)
    cache_vt = cache_v.transpose(0, 1, 3, 4, 2).reshape(depth, n_pool, KV_WIDTH, page)
    lb_param = hg_lower_bound.astype(F32)
    tn_pad = -(-tn // HG_SUB) * HG_SUB

    xp = x_prompt.reshape(b * t, d)
    xs = x_sample.reshape(db * tn, d)
    outs = [[] for _ in range(8)]
    for l in range(depth):
        w = _prep_layer(l, ffn1_norm, ffn1_w_gate, ffn1_w_up, ffn1_w_down, mix_norm, w_in, q_norm, k_norm,
                        hg_norm, w_branch_attn, w_branch_hgrn, w_out, ffn2_norm, ffn2_w_gate, ffn2_w_up,
                        ffn2_w_down)
        xp = _ffn(xp, *w["ffn1"])
        qx, k, v, vb, qc, ki, wi, ph, pg, kt, kct = _proj(xp, b, t, 0, w["proj"], transposed=True)
        att = _dsa_prompt(qx.reshape(b, t, -1), qc.reshape(b, t, -1), wi.reshape(b, t, -1),
                          kct, kt, vb.reshape(b, t, -1), w["sel"])
        rec, s_p = _hgrn(ph.reshape(b, t, -1), lb_param, jnp.zeros((b, HG_HEADS, HG_KDIM, HG_VDIM), F32),
                         w["hg_norm"], l, t)
        xp = _merge(xp, att.reshape(b * t, -1), rec.reshape(b * t, -1), pg, *w["merge"])
        xp = _ffn(xp, *w["ffn2"])
        outs[0].append(k.reshape(b, t, KV_HEADS, HEAD_DIM))
        outs[1].append(v.reshape(b, t, KV_HEADS, HEAD_DIM))
        outs[2].append(ki.reshape(b, t, IDX_DIM))
        outs[3].append(s_p.astype(x_prompt.dtype))

        xs = _ffn(xs, *w["ffn1"])
        qx, k, v, vb, qc, ki, wi, ph, pg, kb, kc = _proj(xs, db, tn, past, w["proj"], transposed=False)
        rows = tn * N_HEADS
        qx3 = qx.reshape(db, rows, MXU_DIM)
        qc3 = qc.reshape(db, rows, MXU_DIM)
        wb3 = jnp.broadcast_to(wi[:, :IDX_HEADS].reshape(db, rows, 1), (db, rows, LANES))
        pad_new = lambda a: jnp.pad(a.reshape(db, tn, -1), ((0, 0), (0, page - tn), (0, 0))).transpose(0, 2, 1)
        o_s = _dsa_sample(page_table, qx3, qc3, wb3, pad_new(kc), pad_new(kb), pad_new(vb),
                          cache_ikt, cache_kt, cache_vt, l)
        o_s = o_s.reshape(db, tn, KV_HEADS, N_HEADS // KV_HEADS, KV_HEADS, HEAD_DIM)
        att_s = jnp.stack([o_s[:, :, g, :, g, :] for g in range(KV_HEADS)], axis=2)
        att_s = att_s.reshape(db * tn, ATT_WIDTH).astype(BF16)
        ph3 = jnp.pad(ph.reshape(db, tn, -1), ((0, 0), (0, tn_pad - tn), (0, 0)))
        rec, s_s = _hgrn(ph3, lb_param, state_hgrn[l], w["hg_norm"], l, tn)
        rec = rec[:, :tn].reshape(db * tn, HG_WIDTH)
        xs = _merge(xs, att_s, rec, pg, *w["merge"])
        xs = _ffn(xs, *w["ffn2"])
        outs[4].append(k.reshape(db, tn, KV_HEADS, HEAD_DIM))
        outs[5].append(v.reshape(db, tn, KV_HEADS, HEAD_DIM))
        outs[6].append(ki.reshape(db, tn, IDX_DIM))
        outs[7].append(s_s.astype(state_hgrn.dtype))

    return (xp.reshape(b, t, d), xs.reshape(db, tn, d)) + tuple(jnp.stack(o) for o in outs)
```

```python
import functools

import numpy as np
import jax
import jax.numpy as jnp
from jax import lax
from jax.experimental import pallas as pl
from jax.experimental.pallas import tpu as pltpu

F32, BF16, I32 = jnp.float32, jnp.bfloat16, jnp.int32

N_HEADS, KV_HEADS, HEAD_DIM = 8, 4, 64
IDX_HEADS, IDX_DIM, TOPK_MAX = 8, 64, 256
HG_HEADS, HG_KDIM, HG_VDIM = 4, 128, 128
ATT_WIDTH = N_HEADS * HEAD_DIM
KV_WIDTH = KV_HEADS * HEAD_DIM
HG_WIDTH = HG_HEADS * HG_KDIM
ROPE_THETA = 10000.0
NORM_EPS = 1e-6

LANES = 128
MXU_DIM = 256
VMEM_LIMIT_BYTES = 56 * 2**20
INT_MIN = -2**31
MASKED_MAX = -1e30
LOG2E = 1.4426950408889634

ROW_TILE = 512
PROJ_ROW_TILE = 256
FFN_CHUNK = 256
HG_STEP = 256
HG_CHUNK = 64
HG_SAFE_DECAY = 80.0
HG_SUB = 16
Q_TILE = 256
K_CHUNK = 512
ATT_ROWS = 128
SAMPLE_CHUNK = 2048


def _params(n_axes):
    return pltpu.CompilerParams(dimension_semantics=("arbitrary",) * n_axes,
                                vmem_limit_bytes=VMEM_LIMIT_BYTES)


def _resident(shape, n_axes):
    zeros = (0,) * len(shape)
    if n_axes == 1:
        return pl.BlockSpec(shape, lambda i: zeros, pipeline_mode=pl.Buffered(1))
    return pl.BlockSpec(shape, lambda i, j: zeros, pipeline_mode=pl.Buffered(1))


def _rms(x, gain):
    return x * lax.rsqrt(jnp.mean(x * x, axis=-1, keepdims=True) + NORM_EPS) * gain


def _sigmoid(x):
    return 1.0 / (1.0 + jnp.exp(-x))


def _dot(a, b):
    return jnp.dot(a, b, preferred_element_type=F32)


def _dot_nt(a, b):
    return lax.dot_general(a, b, (((1,), (1,)), ((), ())), preferred_element_type=F32)


def _dot_tn(a, b):
    return lax.dot_general(a, b, (((0,), (0,)), ((), ())), preferred_element_type=F32)


def _split2(x):
    hi = x.astype(BF16)
    return hi, (x - hi.astype(F32)).astype(BF16)


def _ffn_kernel(x_ref, g_ref, wg_ref, wu_ref, wd_ref, o_ref):
    x = x_ref[...]
    h = _rms(x, g_ref[...]).astype(BF16)
    d_ff = wg_ref.shape[1]
    acc = jnp.zeros(x.shape, F32)
    for c in range(d_ff // FFN_CHUNK):
        cols = slice(c * FFN_CHUNK, (c + 1) * FFN_CHUNK)
        g = _dot(h, wg_ref[:, cols])
        u = _dot(h, wu_ref[:, cols])
        a = (g * _sigmoid(g) * u).astype(BF16)
        acc = acc + _dot(a, wd_ref[cols, :])
    o_ref[...] = x + 0.5 * acc


def _ffn(x, norm, wg, wu, wd):
    n, d = x.shape
    tm = min(ROW_TILE, n)
    assert n % tm == 0 and wg.shape[1] % FFN_CHUNK == 0
    return pl.pallas_call(
        _ffn_kernel,
        grid=(n // tm,),
        in_specs=[pl.BlockSpec((tm, d), lambda i: (i, 0)), _resident((1, d), 1),
                  _resident(wg.shape, 1), _resident(wu.shape, 1), _resident(wd.shape, 1)],
        out_specs=pl.BlockSpec((tm, d), lambda i: (i, 0)),
        out_shape=jax.ShapeDtypeStruct((n, d), F32),
        compiler_params=_params(1),
        name="ffn",
    )(x, norm.reshape(1, d), wg, wu, wd)


def _rope(y, cos, sin):
    n = y.shape[1]
    reps = n // LANES
    c = jnp.tile(cos, (1, reps)) if reps > 1 else cos
    s = jnp.tile(sin, (1, reps)) if reps > 1 else sin
    lane = lax.broadcasted_iota(I32, y.shape, 1)
    first_half = (lane & (HEAD_DIM // 2)) == 0
    partner = jnp.where(first_half, pltpu.roll(y, n - HEAD_DIM // 2, axis=1),
                        pltpu.roll(y, HEAD_DIM // 2, axis=1))
    return y * c + partner * s


def _proj_kernel(x_ref, gn_ref, wq_ref, wk_ref, wv_ref, wqi_ref, wki_ref, wwi_ref, wh_ref, wg_ref,
                 bd_ref, pq_ref, qn_ref, kn_ref, cos_ref, sin_ref,
                 qx_ref, k_ref, v_ref, vb_ref, qc_ref, ki_ref, wi_ref, ph_ref, pg_ref, kb_ref, kc_ref,
                 *, transposed):
    h = _rms(x_ref[...], gn_ref[...]).astype(BF16)
    cos, sin = cos_ref[...], sin_ref[...]

    def head_norm(y, gain):
        n = y.shape[1]
        hi, lo = _split2(y * y)
        bd = bd_ref[:n, :n]
        ss = _dot(hi, bd) + _dot(lo, bd)
        return y * lax.rsqrt(ss * (1.0 / HEAD_DIM) + NORM_EPS) * gain

    q = _rope(head_norm(_dot(h, wq_ref[...]), qn_ref[...]), cos, sin) * (LOG2E * HEAD_DIM ** -0.5)
    qx_ref[...] = _dot(q.astype(BF16), pq_ref[...]).astype(BF16)

    k = _rope(head_norm(_dot(h, wk_ref[...]), kn_ref[...]), cos, sin)
    k_ref[...] = k
    v = _dot(h, wv_ref[...])
    v_ref[...] = v
    vb_ref[...] = v.astype(BF16)

    qi = _rope(_dot(h, wqi_ref[...]), cos, sin) * (IDX_DIM ** -0.5)
    lane = lax.broadcasted_iota(I32, qi.shape, 1) & (MXU_DIM - 1)
    is_lo = (lane >= IDX_DIM) & (lane < 2 * IDX_DIM)
    qc_ref[...] = jnp.where(is_lo, qi - qi.astype(BF16).astype(F32), qi).astype(BF16)

    ki = _rope(_dot(h, wki_ref[...]), cos, sin)
    ki_ref[...] = ki[:, :IDX_DIM]
    lane = lax.broadcasted_iota(I32, ki.shape, 1)
    is_lo = (lane >= 2 * IDX_DIM) & (lane < 3 * IDX_DIM)
    kc = jnp.where(is_lo, ki - ki.astype(BF16).astype(F32), ki)

    if transposed:
        kb_ref[0] = k.T.astype(BF16)
        kc_ref[0] = kc.T.astype(BF16)
    else:
        kb_ref[...] = k.astype(BF16)
        kc_ref[...] = kc.astype(BF16)

    wi_ref[...] = _dot(h, wwi_ref[...]) * (IDX_HEADS ** -0.5)
    ph_ref[...] = _dot(h, wh_ref[...])
    pg_ref[...] = _dot(h, wg_ref[...])


def _proj(x, batch, seq, pos0, w, transposed):
    n, d = x.shape
    tm = min(PROJ_ROW_TILE, n)
    assert n % tm == 0
    lane = np.arange(LANES)
    inv = ROPE_THETA ** (-(lane % (HEAD_DIM // 2)).astype(np.float32) * 2.0 / HEAD_DIM)
    sign = np.where((lane % HEAD_DIM) < HEAD_DIM // 2, -1.0, 1.0).astype(np.float32)
    ang = (pos0 + jnp.arange(seq)).astype(F32)[:, None] * jnp.asarray(inv)[None, :]
    cos_t, sin_t = jnp.cos(ang), jnp.sin(ang) * jnp.asarray(sign)[None, :]
    if seq >= tm:
        assert seq % tm == 0
        per_seq = seq // tm
        tab_map = lambda i: (i % per_seq, 0)
    else:
        assert tm % seq == 0
        cos_t, sin_t = jnp.tile(cos_t, (tm // seq, 1)), jnp.tile(sin_t, (tm // seq, 1))
        tab_map = lambda i: (0, 0)

    row = lambda width: pl.BlockSpec((tm, width), lambda i: (i, 0))
    if transposed:
        per_seq = seq // tm
        kt_spec = pl.BlockSpec((1, MXU_DIM, tm), lambda i: (i // per_seq, 0, i % per_seq))
        kt_shape = jax.ShapeDtypeStruct((batch, MXU_DIM, seq), BF16)
    else:
        kt_spec = row(MXU_DIM)
        kt_shape = jax.ShapeDtypeStruct((n, MXU_DIM), BF16)

    weights = [w["wq"], w["wk"], w["wv"], w["wqi"], w["wki"], w["wwi"], w["wh"], w["wg"],
               w["bd"], w["pq"], w["qn"], w["kn"]]
    out_shape = [
        jax.ShapeDtypeStruct((n, N_HEADS * MXU_DIM), BF16),
        jax.ShapeDtypeStruct((n, KV_WIDTH), F32),
        jax.ShapeDtypeStruct((n, KV_WIDTH), F32),
        jax.ShapeDtypeStruct((n, KV_WIDTH), BF16),
        jax.ShapeDtypeStruct((n, IDX_HEADS * MXU_DIM), BF16),
        jax.ShapeDtypeStruct((n, IDX_DIM), F32),
        jax.ShapeDtypeStruct((n, LANES), F32),
        jax.ShapeDtypeStruct((n, 4 * HG_WIDTH), F32),
        jax.ShapeDtypeStruct((n, 2 * d), F32),
        kt_shape,
        kt_shape,
    ]
    out_specs = [row(N_HEADS * MXU_DIM), row(KV_WIDTH), row(KV_WIDTH), row(KV_WIDTH),
                 row(IDX_HEADS * MXU_DIM), row(IDX_DIM), row(LANES), row(4 * HG_WIDTH), row(2 * d),
                 kt_spec, kt_spec]
    return pl.pallas_call(
        functools.partial(_proj_kernel, transposed=transposed),
        grid=(n // tm,),
        in_specs=[row(d), _resident((1, d), 1)] + [_resident(a.shape, 1) for a in weights]
                 + [pl.BlockSpec((tm, LANES), tab_map), pl.BlockSpec((tm, LANES), tab_map)],
        out_specs=out_specs,
        out_shape=out_shape,
        compiler_params=_params(1),
        name="proj",
    )(x, w["mix_norm"], *weights, cos_t, sin_t)


def _hgrn_kernel(ph_ref, lbp_ref, s0_ref, gn_ref, tri_ref, rec_ref, s_ref, st_ref, o_ref,
                 *, layer, step, chunk, sub, t_valid):
    j = pl.program_id(1)

    @pl.when(j == 0)
    def _():
        for hh in range(HG_HEADS):
            st_ref[hh] = s0_ref[0, hh].T

    lbp = lbp_ref[...]
    e = jnp.exp(lbp - jnp.max(lbp, axis=0, keepdims=True))
    lb_all = jnp.sum(e[:layer + 1], axis=0, keepdims=True) / jnp.sum(e, axis=0, keepdims=True)

    tri = tri_ref[...]
    row_c = lax.broadcasted_iota(I32, (chunk, HG_KDIM), 0)
    row_s = lax.broadcasted_iota(I32, (sub, HG_KDIM), 0)
    causal = (lax.broadcasted_iota(I32, (chunk, chunk), 0) >= lax.broadcasted_iota(I32, (chunk, chunk), 1))
    w = HG_WIDTH

    def chunk_body(ci, carry):
        r0 = pl.multiple_of(ci * chunk, chunk)
        rows = pl.ds(r0, chunk)
        valid = (j * step + r0 + row_c) < t_valid
        heads = []
        worst = jnp.float32(0.0)
        for hh in range(HG_HEADS):
            lanes = slice(hh * HG_KDIM, (hh + 1) * HG_KDIM)
            hq = ph_ref[0, rows, hh * HG_KDIM:(hh + 1) * HG_KDIM]
            z = ph_ref[0, rows, w + hh * HG_KDIM:w + (hh + 1) * HG_KDIM]
            v = ph_ref[0, rows, 2 * w + hh * HG_VDIM:2 * w + (hh + 1) * HG_VDIM]
            hg = ph_ref[0, rows, 3 * w + hh * HG_VDIM:3 * w + (hh + 1) * HG_VDIM]
            lb = lb_all[:, lanes]
            q = hq * _sigmoid(hq) * (HG_KDIM ** -0.5)
            logf = jnp.where(valid, jnp.log(lb + (1.0 - lb) * _sigmoid(z)), 0.0)
            kk = jnp.where(valid, (1.0 - lb) * _sigmoid(-z), 0.0)
            l1 = logf.astype(BF16)
            r1 = logf - l1.astype(F32)
            l2 = r1.astype(BF16)
            l3 = (r1 - l2.astype(F32)).astype(BF16)
            g = _dot(tri, l1) + _dot(tri, l2) + _dot(tri, l3)
            heads.append((q, kk, v, hg, g))
            worst = jnp.maximum(worst, jnp.max(-g[chunk - 1:chunk, :]))

        def finish(hh, o, kk, v, hg, g):
            g_last = g[chunk - 1:chunk, :]
            k_end = (kk * jnp.exp(g_last - g)).astype(BF16)
            st_ref[hh] = jnp.exp(g_last) * st_ref[hh] + _dot_tn(v.astype(BF16), k_end)
            gate = hg * _sigmoid(hg)
            rec_ref[0, rows, hh * HG_VDIM:(hh + 1) * HG_VDIM] = (_rms(o, gn_ref[...]) * gate).astype(BF16)

        @pl.when(worst <= HG_SAFE_DECAY)
        def _():
            for hh, (q, kk, v, hg, g) in enumerate(heads):
                qg = (q * jnp.exp(g)).astype(BF16)
                a = jnp.where(causal, _dot_nt(qg, (kk * jnp.exp(-g)).astype(BF16)), 0.0)
                o = _dot_nt(qg, st_ref[hh].astype(BF16)) + _dot(a.astype(BF16), v.astype(BF16))
                finish(hh, o, kk, v, hg, g)

        @pl.when(worst > HG_SAFE_DECAY)
        def _():
            for hh, (q, kk, v, hg, g) in enumerate(heads):
                vb = v.astype(BF16)
                o_ref[hh] = _dot_nt((q * jnp.exp(g)).astype(BF16), st_ref[hh].astype(BF16))
                for i in range(chunk // sub):
                    b0 = i * sub
                    gi, qi, ki_, vi = g[b0:b0 + sub], q[b0:b0 + sub], kk[b0:b0 + sub], v[b0:b0 + sub]
                    if i > 0:
                        ref = g[b0 - 1:b0, :]
                        q_rel = (qi * jnp.exp(gi - ref)).astype(BF16)
                        k_rel = (kk[:b0] * jnp.exp(ref - g[:b0])).astype(BF16)
                        a = _dot_nt(q_rel, k_rel)
                        o_ref[hh, b0:b0 + sub, :] += _dot(a.astype(BF16), vb[:b0])
                    for t in range(sub):
                        dec = jnp.exp(jnp.where(row_s <= t, gi[t:t + 1, :] - gi, -jnp.inf))
                        a_t = jnp.sum(qi[t:t + 1, :] * dec * ki_, axis=1, keepdims=True)
                        o_ref[hh, b0 + t:b0 + t + 1, :] += jnp.sum(a_t * vi, axis=0, keepdims=True)
                finish(hh, o_ref[hh], kk, v, hg, g)
        return carry

    lax.fori_loop(0, step // chunk, chunk_body, 0)

    @pl.when(j == pl.num_programs(1) - 1)
    def _():
        for hh in range(HG_HEADS):
            s_ref[0, hh] = st_ref[hh].T


def _hgrn(ph, lb_param, s0, hg_norm, layer, t_valid):
    b, t, _ = ph.shape
    step = min(HG_STEP, t)
    chunk = min(HG_CHUNK, step)
    sub = min(HG_SUB, chunk)
    assert t % step == 0 and step % chunk == 0 and chunk % sub == 0
    tri = jnp.asarray(np.tril(np.ones((chunk, chunk), np.float32)), BF16)
    return pl.pallas_call(
        functools.partial(_hgrn_kernel, layer=layer, step=step, chunk=chunk, sub=sub, t_valid=t_valid),
        grid=(b, t // step),
        in_specs=[pl.BlockSpec((1, step, 4 * HG_WIDTH), lambda i, j: (i, j, 0)),
                  _resident(lb_param.shape, 2),
                  pl.BlockSpec((1, HG_HEADS, HG_KDIM, HG_VDIM), lambda i, j: (i, 0, 0, 0)),
                  _resident((1, HG_VDIM), 2), _resident(tri.shape, 2)],
        out_specs=[pl.BlockSpec((1, step, HG_WIDTH), lambda i, j: (i, j, 0)),
                   pl.BlockSpec((1, HG_HEADS, HG_KDIM, HG_VDIM), lambda i, j: (i, 0, 0, 0))],
        out_shape=[jax.ShapeDtypeStruct((b, t, HG_WIDTH), BF16),
                   jax.ShapeDtypeStruct((b, HG_HEADS, HG_KDIM, HG_VDIM), F32)],
        scratch_shapes=[pltpu.VMEM((HG_HEADS, HG_VDIM, HG_KDIM), F32),
                        pltpu.VMEM((HG_HEADS, chunk, HG_VDIM), F32)],
        compiler_params=_params(2),
        name="hgrn",
    )(ph, lb_param, s0, hg_norm.reshape(1, HG_VDIM), tri)


def _sortable_key(score):
    bits = pltpu.bitcast(score, I32)
    return jnp.where(bits < 0, INT_MIN - bits, bits)


def _kth_largest(count_ge, topk, shape):
    zero = jnp.zeros(shape, I32)
    c = count_ge(zero)
    ok = c >= topk
    tau = jnp.where(ok, zero, jnp.full(shape, INT_MIN, I32))
    n_ge = jnp.where(ok, c, 0)

    def step(it, carry):
        tau, n_ge = carry
        cand = tau | lax.shift_left(jnp.int32(1), 30 - it)
        c = count_ge(cand)
        ok = c >= topk
        return jnp.where(ok, cand, tau), jnp.where(ok, c, n_ge)

    return lax.fori_loop(0, 31, step, (tau, n_ge))


def _dsa_prompt_kernel(qx_ref, qc_ref, wi_ref, kct_ref, kt_ref, v_ref, tri_ref, sel_ref, o_ref,
                       keys_ref, keyst_ref, aq_ref, ac_ref, wb_ref, m_ref, l_ref, acc_ref, lg_ref, p_ref, bias_ref,
                       *, tq, ck, topk):
    i = pl.program_id(1)
    row0 = i * tq
    n_chunks = (row0 + tq + ck - 1) // ck
    n_sub = ck // LANES
    rows_q = N_HEADS * tq

    for h in range(IDX_HEADS):
        ac_ref[h * tq:(h + 1) * tq, :] = qc_ref[0, :, h * MXU_DIM:(h + 1) * MXU_DIM]
        aq_ref[h * tq:(h + 1) * tq, :] = qx_ref[0, :, h * MXU_DIM:(h + 1) * MXU_DIM]
        wb_ref[h] = jnp.broadcast_to(wi_ref[0, :, h:h + 1], (tq, LANES))

    row_id = row0 + lax.broadcasted_iota(I32, (tq, ck), 0)
    col_id = lax.broadcasted_iota(I32, (tq, ck), 1)

    def score_body(c, carry):
        col0 = pl.multiple_of(c * ck, ck)
        s = _dot(ac_ref[...], kct_ref[0, :, pl.ds(col0, ck)])
        tot = jnp.zeros((tq, ck), F32)
        for h in range(IDX_HEADS):
            tot = tot + jnp.maximum(s[h * tq:(h + 1) * tq, :], 0.0) * jnp.tile(wb_ref[h], (1, n_sub))
        key = jnp.where(col0 + col_id <= row_id, _sortable_key(tot), INT_MIN)
        keys_ref[:, pl.ds(col0, ck)] = key
        keyst_ref[pl.ds(col0, ck), :] = key.T
        return carry

    lax.fori_loop(0, n_chunks, score_body, 0)

    def count(cmp, cand):
        def body(c, cnt):
            r0 = pl.multiple_of(c * ck, ck)
            hit = jnp.where(cmp(keyst_ref[pl.ds(r0, ck), :], cand), 1, 0)
            return cnt + jnp.sum(hit.reshape(ck // 8, 8, tq), axis=0)
        cnt = lax.fori_loop(0, n_chunks, body, jnp.zeros((8, tq), I32))
        return jnp.sum(cnt, axis=0, keepdims=True)

    def per_row(x):
        return jnp.broadcast_to(x, (LANES, tq)).T

    tau, n_ge = _kth_largest(lambda cand: count(lambda a, b: a >= b, cand), topk, (1, tq))
    tau_lanes = jnp.maximum(tau, INT_MIN + 1)
    tau_sel = per_row(tau_lanes)
    has_ties = jnp.max(n_ge) > topk

    m_ref[...] = jnp.full(m_ref.shape, MASKED_MAX, F32)
    l_ref[...] = jnp.zeros(l_ref.shape, F32)
    acc_ref[...] = jnp.zeros(acc_ref.shape, F32)

    def attend(ranked):
        tau_t = jnp.tile(tau_sel, (1, n_sub))
        if ranked:
            need_eq = per_row((topk - count(lambda a, b: a > b, tau_lanes)).astype(F32))[:, 0:1]

        def body(c, eq_seen):
            col0 = pl.multiple_of(c * ck, ck)
            blk = keys_ref[:, pl.ds(col0, ck)]
            if ranked:
                eq = blk == tau_t
                eq_f = jnp.where(eq, 1.0, 0.0)
                before = _dot(eq_f.astype(BF16), tri_ref[...]) + eq_seen
                rank = jnp.where(eq, before, jnp.where(blk > tau_t, -1.0, jnp.inf))
                bias_ref[...] = jnp.where(rank < need_eq, 0.0, -jnp.inf)
                eq_seen = eq_seen + jnp.sum(eq_f, axis=1, keepdims=True)
            else:
                bias_ref[...] = jnp.where(blk >= tau_t, 0.0, -jnp.inf)
            for g in range(rows_q // ATT_ROWS):
                rows = slice(g * ATT_ROWS, (g + 1) * ATT_ROWS)
                off = (g * ATT_ROWS) % tq
                lg_ref[rows, :] = (_dot(aq_ref[rows, :], kt_ref[0, :, pl.ds(col0, ck)])
                                   + bias_ref[off:off + ATT_ROWS, :])
                m_old = m_ref[rows, :]
                m_new = jnp.maximum(m_old, jnp.max(lg_ref[rows, :], axis=1, keepdims=True))
                m_ref[rows, :] = m_new
                alpha = jnp.exp2(m_old - m_new)
                p = jnp.exp2(lg_ref[rows, :] - jnp.tile(m_new, (1, n_sub)))
                p_ref[rows, :] = p.astype(BF16)
                l_ref[rows, :] = alpha * l_ref[rows, :] + jnp.sum(p, axis=1, keepdims=True)
                acc_ref[rows, :] = (jnp.tile(alpha, (1, KV_WIDTH // LANES)) * acc_ref[rows, :]
                                    + _dot(p_ref[rows, :], v_ref[0, pl.ds(col0, ck), :]))
            return eq_seen
        lax.fori_loop(0, n_chunks, body, jnp.zeros((tq, 1), F32))

    @pl.when(has_ties)
    def _():
        attend(True)

    @pl.when(jnp.logical_not(has_ties))
    def _():
        attend(False)

    out = jnp.zeros((tq, ATT_WIDTH), F32)
    for h in range(N_HEADS):
        rows = slice(h * tq, (h + 1) * tq)
        o_h = acc_ref[rows, :] / jnp.tile(l_ref[rows, :], (1, KV_WIDTH // LANES))
        out = out + _dot(o_h.astype(BF16), sel_ref[h])
    o_ref[0] = out.astype(BF16)


def _dsa_prompt(qx, qc, wi, kct, kt, vb, sel):
    b, t, _ = qx.shape
    tq, ck = min(Q_TILE, t), min(K_CHUNK, t)
    assert t % tq == 0 and t % ck == 0 and ck % tq == 0
    topk = min(TOPK_MAX, t // 4)
    tri = jnp.asarray(np.triu(np.ones((ck, ck), np.float32), 1), BF16)
    tile = lambda width: pl.BlockSpec((1, tq, width), lambda i, j: (i, j, 0))
    whole = lambda rows, cols: pl.BlockSpec((1, rows, cols), lambda i, j: (i, 0, 0),
                                            pipeline_mode=pl.Buffered(1))
    return pl.pallas_call(
        functools.partial(_dsa_prompt_kernel, tq=tq, ck=ck, topk=topk),
        grid=(b, t // tq),
        in_specs=[tile(N_HEADS * MXU_DIM), tile(IDX_HEADS * MXU_DIM), tile(LANES),
                  whole(MXU_DIM, t), whole(MXU_DIM, t), whole(t, KV_WIDTH),
                  _resident(tri.shape, 2), _resident(sel.shape, 2)],
        out_specs=tile(ATT_WIDTH),
        out_shape=jax.ShapeDtypeStruct((b, t, ATT_WIDTH), BF16),
        scratch_shapes=[pltpu.VMEM((tq, t), I32),
                        pltpu.VMEM((t, tq), I32),
                        pltpu.VMEM((N_HEADS * tq, MXU_DIM), BF16),
                        pltpu.VMEM((IDX_HEADS * tq, MXU_DIM), BF16),
                        pltpu.VMEM((IDX_HEADS, tq, LANES), F32),
                        pltpu.VMEM((N_HEADS * tq, LANES), F32),
                        pltpu.VMEM((N_HEADS * tq, LANES), F32),
                        pltpu.VMEM((N_HEADS * tq, KV_WIDTH), F32),
                        pltpu.VMEM((N_HEADS * tq, ck), F32),
                        pltpu.VMEM((N_HEADS * tq, ck), BF16),
                        pltpu.VMEM((tq, ck), F32)],
        compiler_params=_params(2),
        name="dsa_prompt",
    )(qx, qc, wi, kct, kt, vb, tri, sel)


def _dsa_sample_kernel(pt_ref, qx_ref, qc_ref, wb_ref, kcn_ref, kn_ref, vn_ref, tri_ref,
                       cik_hbm, ck_hbm, cv_hbm, o_ref,
                       ik_buf, k_buf, v_buf, keys_ref, bias_ref, lg_ref, sem,
                       *, layer, n_pages, page, chunk, tn, topk):
    b = pl.program_id(0)
    nb = pl.num_programs(0)
    past = n_pages * page
    width = past + LANES
    rows = tn * N_HEADS

    def copies(sample, slot, p):
        pg = pt_ref[sample, p]
        dst = pl.ds(pl.multiple_of(p * page, page), page)
        return (pltpu.make_async_copy(cik_hbm.at[layer, pg], ik_buf.at[slot, :, dst], sem.at[slot, 0]),
                pltpu.make_async_copy(ck_hbm.at[layer, pg], k_buf.at[slot, :, dst], sem.at[slot, 1]),
                pltpu.make_async_copy(cv_hbm.at[layer, pg], v_buf.at[slot, :, dst], sem.at[slot, 2]))

    def fetch(sample, slot):
        def body(p, carry):
            for cp in copies(sample, slot, p):
                cp.start()
            return carry
        lax.fori_loop(0, n_pages, body, 0)

    def wait(sample, slot):
        def body(p, carry):
            for cp in copies(sample, slot, p):
                cp.wait()
            return carry
        lax.fori_loop(0, n_pages, body, 0)

    slot = b % 2

    @pl.when(b == 0)
    def _():
        fetch(0, 0)

    @pl.when(b + 1 < nb)
    def _():
        fetch(b + 1, 1 - slot)

    wait(b, slot)

    qc = qc_ref[0]
    qx = qx_ref[0]
    wb = wb_ref[0]
    tok = lax.broadcasted_iota(I32, (tn, LANES), 0)
    col = lax.broadcasted_iota(I32, (tn, LANES), 1)

    def head_sum(s):
        n = s.shape[1]
        r = jnp.maximum(s, 0.0) * jnp.tile(wb, (1, n // LANES))
        return jnp.sum(r.reshape(tn, N_HEADS, n), axis=1)

    keys_ref[...] = jnp.full(keys_ref.shape, INT_MIN, I32)

    zero_rows = jnp.zeros((MXU_DIM - 3 * IDX_DIM, chunk), BF16)
    for c in range(past // chunk):
        cols = slice(c * chunk, (c + 1) * chunk)
        hi, lo = _split2(ik_buf[slot, :, cols])
        kct = jnp.concatenate([hi, hi, lo, zero_rows], axis=0)
        keys_ref[0:tn, cols] = _sortable_key(head_sum(_dot(qc, kct)))
    s_new = head_sum(_dot(qc, kcn_ref[0]))
    keys_ref[0:tn, past:width] = jnp.where(col <= tok, _sortable_key(s_new), INT_MIN)

    def count_ge(cand):
        return jnp.sum(jnp.where(keys_ref[...] >= cand, 1, 0), axis=1, keepdims=True)

    tau, n_ge = _kth_largest(count_ge, topk, (8, 1))
    tau_sel = jnp.maximum(tau, INT_MIN + 1)
    has_ties = jnp.max(n_ge) > topk

    @pl.when(jnp.logical_not(has_ties))
    def _():
        bias_ref[...] = jnp.where(keys_ref[...] >= tau_sel, 0.0, -jnp.inf)

    @pl.when(has_ties)
    def _():
        n_gt = jnp.sum(jnp.where(keys_ref[...] > tau_sel, 1, 0), axis=1, keepdims=True)
        need_eq = (topk - n_gt).astype(F32)

        def body(u, eq_seen):
            c0 = pl.multiple_of(u * LANES, LANES)
            blk = keys_ref[:, pl.ds(c0, LANES)]
            eq = blk == tau_sel
            eq_f = jnp.where(eq, 1.0, 0.0)
            before = _dot(eq_f.astype(BF16), tri_ref[...]) + eq_seen
            rank = jnp.where(eq, before, jnp.where(blk > tau_sel, -1.0, jnp.inf))
            bias_ref[:, pl.ds(c0, LANES)] = jnp.where(rank < need_eq, 0.0, -jnp.inf)
            return eq_seen + jnp.sum(eq_f, axis=1, keepdims=True)

        lax.fori_loop(0, width // LANES, body, jnp.zeros((8, 1), F32))

    def expand(x):
        n = x.shape[1]
        return jnp.broadcast_to(x[0:tn].reshape(tn, 1, n), (tn, N_HEADS, n)).reshape(rows, n)

    for c in range(past // chunk):
        cols = slice(c * chunk, (c + 1) * chunk)
        lg_ref[:, cols] = _dot(qx, k_buf[slot, :, cols].astype(BF16)) + expand(bias_ref[:, cols])
    lg_ref[:, past:width] = _dot(qx, kn_ref[0]) + expand(bias_ref[:, past:width])
    lg = lg_ref[...]
    p = jnp.exp2(lg - jnp.max(lg, axis=1, keepdims=True))
    l = jnp.sum(p, axis=1, keepdims=True)
    lg_ref[...] = p
    acc = _dot_nt(lg_ref[:, past:width].astype(BF16), vn_ref[0])
    for c in range(past // chunk):
        cols = slice(c * chunk, (c + 1) * chunk)
        acc = acc + _dot_nt(lg_ref[:, cols].astype(BF16), v_buf[slot, :, cols].astype(BF16))
    o_ref[0] = acc / l


def _dsa_sample(page_table, qx, qc, wb, kcn, kn, vn, cache_ik, cache_k, cache_v, layer):
    db, rows, _ = qx.shape
    tn = rows // N_HEADS
    n_pages = page_table.shape[1]
    page = cache_ik.shape[3]
    past = n_pages * page
    chunk = min(SAMPLE_CHUNK, past)
    assert page == LANES and tn <= 8 and past % chunk == 0
    tri = jnp.asarray(np.triu(np.ones((page, page), np.float32), 1), BF16)
    topk = min(TOPK_MAX, (past + tn) // 4)
    per = lambda r, c: pl.BlockSpec((1, r, c), lambda i, pt: (i, 0, 0))
    grid_spec = pltpu.PrefetchScalarGridSpec(
        num_scalar_prefetch=1,
        grid=(db,),
        in_specs=[per(rows, MXU_DIM), per(rows, MXU_DIM), per(rows, LANES),
                  per(MXU_DIM, LANES), per(KV_WIDTH, LANES), per(KV_WIDTH, LANES),
                  pl.BlockSpec(tri.shape, lambda i, pt: (0, 0)),
                  pl.BlockSpec(memory_space=pl.ANY), pl.BlockSpec(memory_space=pl.ANY),
                  pl.BlockSpec(memory_space=pl.ANY)],
        out_specs=per(rows, KV_WIDTH),
        scratch_shapes=[pltpu.VMEM((2, IDX_DIM, past), F32),
                        pltpu.VMEM((2, KV_WIDTH, past), F32),
                        pltpu.VMEM((2, KV_WIDTH, past), F32),
                        pltpu.VMEM((8, past + LANES), I32),
                        pltpu.VMEM((8, past + LANES), F32),
                        pltpu.VMEM((rows, past + LANES), F32),
                        pltpu.SemaphoreType.DMA((2, 3))],
    )
    return pl.pallas_call(
        functools.partial(_dsa_sample_kernel, layer=layer, n_pages=n_pages, page=page, chunk=chunk,
                          tn=tn, topk=topk),
        grid_spec=grid_spec,
        out_shape=jax.ShapeDtypeStruct((db, rows, KV_WIDTH), F32),
        compiler_params=_params(1),
        name="dsa_sample",
    )(page_table, qx, qc, wb, kcn, kn, vn, tri, cache_ik, cache_k, cache_v)


def _merge_kernel(x_ref, att_ref, rec_ref, pg_ref, wa_ref, wh_ref, wo_ref, o_ref):
    d = x_ref.shape[1]
    pg = pg_ref[...]
    merged = (_sigmoid(pg[:, :d]) * _dot(att_ref[...], wa_ref[...])
              + _sigmoid(pg[:, d:]) * _dot(rec_ref[...], wh_ref[...]))
    o_ref[...] = x_ref[...] + _dot(merged.astype(BF16), wo_ref[...])


def _merge(x, att, rec, pg, wa, wh, wo):
    n, d = x.shape
    tm = min(ROW_TILE, n)
    row = lambda width: pl.BlockSpec((tm, width), lambda i: (i, 0))
    return pl.pallas_call(
        _merge_kernel,
        grid=(n // tm,),
        in_specs=[row(d), row(ATT_WIDTH), row(HG_WIDTH), row(2 * d),
                  _resident(wa.shape, 1), _resident(wh.shape, 1), _resident(wo.shape, 1)],
        out_specs=row(d),
        out_shape=jax.ShapeDtypeStruct((n, d), F32),
        compiler_params=_params(1),
        name="merge",
    )(x, att, rec, pg, wa, wh, wo)


def _prep_layer(l, ffn1_norm, ffn1_w_gate, ffn1_w_up, ffn1_w_down, mix_norm, w_in, q_norm, k_norm, hg_norm,
                w_branch_attn, w_branch_hgrn, w_out, ffn2_norm, ffn2_w_gate, ffn2_w_up, ffn2_w_down):
    d = w_in.shape[1]
    w = w_in[l]
    off = [0]

    def take(width):
        a = w[:, off[0]:off[0] + width]
        off[0] += width
        return a

    wq, wk, wv = take(ATT_WIDTH), take(KV_WIDTH), take(KV_WIDTH)
    wqi, wki, wwi = take(IDX_HEADS * IDX_DIM), take(IDX_DIM), take(IDX_HEADS)
    wh = take(4 * HG_WIDTH)
    wg = take(2 * d)
    zeros = jnp.zeros((d, IDX_DIM), F32)
    wqi_heads = wqi.reshape(d, IDX_HEADS, IDX_DIM)
    wqi_arr = jnp.concatenate([wqi_heads, wqi_heads, wqi_heads, jnp.zeros_like(wqi_heads)], axis=2)
    wki_arr = jnp.concatenate([wki, wki, wki, zeros], axis=1)
    wwi_arr = jnp.concatenate([wwi, jnp.zeros((d, LANES - IDX_HEADS), F32)], axis=1)
    bd = np.kron(np.eye(ATT_WIDTH // HEAD_DIM, dtype=np.float32), np.ones((HEAD_DIM, HEAD_DIM), np.float32))
    pq = np.zeros((ATT_WIDTH, N_HEADS * MXU_DIM), np.float32)
    sel = np.zeros((N_HEADS, KV_WIDTH, ATT_WIDTH), np.float32)
    for h in range(N_HEADS):
        g = h // (N_HEADS // KV_HEADS)
        for e in range(HEAD_DIM):
            pq[h * HEAD_DIM + e, h * MXU_DIM + g * HEAD_DIM + e] = 1.0
            sel[h, g * HEAD_DIM + e, h * HEAD_DIM + e] = 1.0
    bf = lambda a: a.astype(BF16)
    return dict(
        ffn1=(ffn1_norm[l], bf(ffn1_w_gate[l]), bf(ffn1_w_up[l]), bf(ffn1_w_down[l])),
        ffn2=(ffn2_norm[l], bf(ffn2_w_gate[l]), bf(ffn2_w_up[l]), bf(ffn2_w_down[l])),
        proj=dict(mix_norm=mix_norm[l].reshape(1, d), wq=bf(wq), wk=bf(wk), wv=bf(wv),
                  wqi=bf(wqi_arr.reshape(d, IDX_HEADS * MXU_DIM)), wki=bf(wki_arr), wwi=bf(wwi_arr),
                  wh=bf(wh), wg=bf(wg), bd=jnp.asarray(bd, BF16), pq=jnp.asarray(pq, BF16),
                  qn=jnp.tile(q_norm[l], N_HEADS).reshape(1, ATT_WIDTH),
                  kn=jnp.tile(k_norm[l], KV_HEADS).reshape(1, KV_WIDTH)),
        sel=jnp.asarray(sel, BF16),
        hg_norm=hg_norm[l],
        merge=(bf(w_branch_attn[l]), bf(w_branch_hgrn[l]), bf(w_out[l])),
    )


def kernel(x_prompt, x_sample, cache_k, cache_v, cache_idx_k, state_hgrn, page_table, hg_lower_bound,
           ffn1_norm, ffn1_w_gate, ffn1_w_up, ffn1_w_down, mix_norm, w_in, q_norm, k_norm, hg_norm,
           w_branch_attn, w_branch_hgrn, w_out, ffn2_norm, ffn2_w_gate, ffn2_w_up, ffn2_w_down):
    b, t, d = x_prompt.shape
    db, tn, _ = x_sample.shape
    depth = w_in.shape[0]
    n_pool, page = cache_k.shape[1], cache_k.shape[2]
    past = page_table.shape[1] * page
    cache_ikt = cache_idx_k.transpose(0, 1, 3, 2)
    cache_kt = cache_k.transpose(0, 1, 3, 4, 2).reshape(depth, n_pool, KV_WIDTH, page)
    cache_vt = cache_v.transpose(0, 1, 3, 4, 2).reshape(depth, n_pool, KV_WIDTH, page)
    lb_param = hg_lower_bound.astype(F32)
    tn_pad = -(-tn // HG_SUB) * HG_SUB

    xp = x_prompt.reshape(b * t, d)
    xs = x_sample.reshape(db * tn, d)
    outs = [[] for _ in range(8)]
    for l in range(depth):
        w = _prep_layer(l, ffn1_norm, ffn1_w_gate, ffn1_w_up, ffn1_w_down, mix_norm, w_in, q_norm, k_norm,
                        hg_norm, w_branch_attn, w_branch_hgrn, w_out, ffn2_norm, ffn2_w_gate, ffn2_w_up,
                        ffn2_w_down)
        xp = _ffn(xp, *w["ffn1"])
        qx, k, v, vb, qc, ki, wi, ph, pg, kt, kct = _proj(xp, b, t, 0, w["proj"], transposed=True)
        att = _dsa_prompt(qx.reshape(b, t, -1), qc.reshape(b, t, -1), wi.reshape(b, t, -1),
                          kct, kt, vb.reshape(b, t, -1), w["sel"])
        rec, s_p = _hgrn(ph.reshape(b, t, -1), lb_param, jnp.zeros((b, HG_HEADS, HG_KDIM, HG_VDIM), F32),
                         w["hg_norm"], l, t)
        xp = _merge(xp, att.reshape(b * t, -1), rec.reshape(b * t, -1), pg, *w["merge"])
        xp = _ffn(xp, *w["ffn2"])
        outs[0].append(k.reshape(b, t, KV_HEADS, HEAD_DIM))
        outs[1].append(v.reshape(b, t, KV_HEADS, HEAD_DIM))
        outs[2].append(ki.reshape(b, t, IDX_DIM))
        outs[3].append(s_p.astype(x_prompt.dtype))

        xs = _ffn(xs, *w["ffn1"])
        qx, k, v, vb, qc, ki, wi, ph, pg, kb, kc = _proj(xs, db, tn, past, w["proj"], transposed=False)
        rows = tn * N_HEADS
        qx3 = qx.reshape(db, rows, MXU_DIM)
        qc3 = qc.reshape(db, rows, MXU_DIM)
        wb3 = jnp.broadcast_to(wi[:, :IDX_HEADS].reshape(db, rows, 1), (db, rows, LANES))
        pad_new = lambda a: jnp.pad(a.reshape(db, tn, -1), ((0, 0), (0, page - tn), (0, 0))).transpose(0, 2, 1)
        o_s = _dsa_sample(page_table, qx3, qc3, wb3, pad_new(kc), pad_new(kb), pad_new(vb),
                          cache_ikt, cache_kt, cache_vt, l)
        o_s = o_s.reshape(db, tn, KV_HEADS, N_HEADS // KV_HEADS, KV_HEADS, HEAD_DIM)
        att_s = jnp.stack([o_s[:, :, g, :, g, :] for g in range(KV_HEADS)], axis=2)
        att_s = att_s.reshape(db * tn, ATT_WIDTH).astype(BF16)
        ph3 = jnp.pad(ph.reshape(db, tn, -1), ((0, 0), (0, tn_pad - tn), (0, 0)))
        rec, s_s = _hgrn(ph3, lb_param, state_hgrn[l], w["hg_norm"], l, tn)
        rec = rec[:, :tn].reshape(db * tn, HG_WIDTH)
        xs = _merge(xs, att_s, rec, pg, *w["merge"])
        xs = _ffn(xs, *w["ffn2"])
        outs[4].append(k.reshape(db, tn, KV_HEADS, HEAD_DIM))
        outs[5].append(v.reshape(db, tn, KV_HEADS, HEAD_DIM))
        outs[6].append(ki.reshape(db, tn, IDX_DIM))
        outs[7].append(s_s.astype(state_hgrn.dtype))

    return (xp.reshape(b, t, d), xs.reshape(db, tn, d)) + tuple(jnp.stack(o) for o in outs)
```

```python
import functools

import numpy as np
import jax
import jax.numpy as jnp
from jax import lax
from jax.experimental import pallas as pl
from jax.experimental.pallas import tpu as pltpu

F32, BF16, I32 = jnp.float32, jnp.bfloat16, jnp.int32

N_HEADS, KV_HEADS, HEAD_DIM = 8, 4, 64
IDX_HEADS, IDX_DIM, TOPK_MAX = 8, 64, 256
HG_HEADS, HG_KDIM, HG_VDIM = 4, 128, 128
ATT_WIDTH = N_HEADS * HEAD_DIM
KV_WIDTH = KV_HEADS * HEAD_DIM
HG_WIDTH = HG_HEADS * HG_KDIM
ROPE_THETA = 10000.0
NORM_EPS = 1e-6

LANES = 128
MXU_DIM = 256
VMEM_LIMIT_BYTES = 56 * 2**20
INT_MIN = -2**31
MASKED_MAX = -1e30
LOG2E = 1.4426950408889634

ROW_TILE = 512
PROJ_ROW_TILE = 256
FFN_CHUNK = 256
HG_STEP = 256
HG_CHUNK = 64
HG_SAFE_DECAY = 80.0
HG_SUB = 16
Q_TILE = 256
K_CHUNK = 512
ATT_ROWS = 256
COUNT_ACC = 32
SAMPLE_CHUNK = 2048


def _params(n_axes):
    return pltpu.CompilerParams(dimension_semantics=("arbitrary",) * n_axes,
                                vmem_limit_bytes=VMEM_LIMIT_BYTES)


def _resident(shape, n_axes):
    zeros = (0,) * len(shape)
    if n_axes == 1:
        return pl.BlockSpec(shape, lambda i: zeros, pipeline_mode=pl.Buffered(1))
    return pl.BlockSpec(shape, lambda i, j: zeros, pipeline_mode=pl.Buffered(1))


def _rms(x, gain):
    return x * lax.rsqrt(jnp.mean(x * x, axis=-1, keepdims=True) + NORM_EPS) * gain


def _sigmoid(x):
    return 1.0 / (1.0 + jnp.exp(-x))


def _dot(a, b):
    return jnp.dot(a, b, preferred_element_type=F32)


def _dot_nt(a, b):
    return lax.dot_general(a, b, (((1,), (1,)), ((), ())), preferred_element_type=F32)


def _dot_tn(a, b):
    return lax.dot_general(a, b, (((0,), (0,)), ((), ())), preferred_element_type=F32)


def _split2(x):
    hi = x.astype(BF16)
    return hi, (x - hi.astype(F32)).astype(BF16)


def _ffn_apply(x, g_ref, wg_ref, wu_ref, wd_ref):
    h = _rms(x, g_ref[...]).astype(BF16)
    d_ff = wg_ref.shape[1]
    acc = jnp.zeros(x.shape, F32)
    for c in range(d_ff // FFN_CHUNK):
        cols = slice(c * FFN_CHUNK, (c + 1) * FFN_CHUNK)
        g = _dot(h, wg_ref[:, cols])
        u = _dot(h, wu_ref[:, cols])
        a = (g * _sigmoid(g) * u).astype(BF16)
        acc = acc + _dot(a, wd_ref[cols, :])
    return x + 0.5 * acc


def _ffn_kernel(x_ref, g_ref, wg_ref, wu_ref, wd_ref, o_ref):
    o_ref[...] = _ffn_apply(x_ref[...], g_ref, wg_ref, wu_ref, wd_ref)


def _ffn(x, norm, wg, wu, wd):
    n, d = x.shape
    tm = min(ROW_TILE, n)
    assert n % tm == 0 and wg.shape[1] % FFN_CHUNK == 0
    return pl.pallas_call(
        _ffn_kernel,
        grid=(n // tm,),
        in_specs=[pl.BlockSpec((tm, d), lambda i: (i, 0)), _resident((1, d), 1),
                  _resident(wg.shape, 1), _resident(wu.shape, 1), _resident(wd.shape, 1)],
        out_specs=pl.BlockSpec((tm, d), lambda i: (i, 0)),
        out_shape=jax.ShapeDtypeStruct((n, d), F32),
        compiler_params=_params(1),
        name="ffn",
    )(x, norm.reshape(1, d), wg, wu, wd)


def _rope(y, cos, sin):
    n = y.shape[1]
    reps = n // LANES
    c = jnp.tile(cos, (1, reps)) if reps > 1 else cos
    s = jnp.tile(sin, (1, reps)) if reps > 1 else sin
    lane = lax.broadcasted_iota(I32, y.shape, 1)
    first_half = (lane & (HEAD_DIM // 2)) == 0
    partner = jnp.where(first_half, pltpu.roll(y, n - HEAD_DIM // 2, axis=1),
                        pltpu.roll(y, HEAD_DIM // 2, axis=1))
    return y * c + partner * s


def _proj_kernel(x_ref, gn_ref, wq_ref, wk_ref, wv_ref, wqi_ref, wki_ref, wwi_ref, wh_ref, wg_ref,
                 bd_ref, pq_ref, qn_ref, kn_ref, cos_ref, sin_ref,
                 qx_ref, k_ref, v_ref, vb_ref, qc_ref, ki_ref, wi_ref, ph_ref, pg_ref, kb_ref, kc_ref,
                 *, transposed):
    h = _rms(x_ref[...], gn_ref[...]).astype(BF16)
    cos, sin = cos_ref[...], sin_ref[...]

    def head_norm(y, gain):
        n = y.shape[1]
        hi, lo = _split2(y * y)
        bd = bd_ref[:n, :n]
        ss = _dot(hi, bd) + _dot(lo, bd)
        return y * lax.rsqrt(ss * (1.0 / HEAD_DIM) + NORM_EPS) * gain

    q = _rope(head_norm(_dot(h, wq_ref[...]), qn_ref[...]), cos, sin) * (LOG2E * HEAD_DIM ** -0.5)
    qx_ref[...] = _dot(q.astype(BF16), pq_ref[...]).astype(BF16)

    k = _rope(head_norm(_dot(h, wk_ref[...]), kn_ref[...]), cos, sin)
    k_ref[...] = k
    v = _dot(h, wv_ref[...])
    v_ref[...] = v
    vb_ref[...] = v.astype(BF16)

    qi = _rope(_dot(h, wqi_ref[...]), cos, sin) * (IDX_DIM ** -0.5)
    lane = lax.broadcasted_iota(I32, qi.shape, 1) & (MXU_DIM - 1)
    is_lo = (lane >= IDX_DIM) & (lane < 2 * IDX_DIM)
    qc_ref[...] = jnp.where(is_lo, qi - qi.astype(BF16).astype(F32), qi).astype(BF16)

    ki = _rope(_dot(h, wki_ref[...]), cos, sin)
    ki_ref[...] = ki[:, :IDX_DIM]
    lane = lax.broadcasted_iota(I32, ki.shape, 1)
    is_lo = (lane >= 2 * IDX_DIM) & (lane < 3 * IDX_DIM)
    kc = jnp.where(is_lo, ki - ki.astype(BF16).astype(F32), ki)

    if transposed:
        kb_ref[0] = k.T.astype(BF16)
        kc_ref[0] = kc.T.astype(BF16)
    else:
        kb_ref[...] = k.astype(BF16)
        kc_ref[...] = kc.astype(BF16)

    wi_ref[...] = _dot(h, wwi_ref[...]) * (IDX_HEADS ** -0.5)
    ph_ref[...] = _dot(h, wh_ref[...])
    pg_ref[...] = _dot(h, wg_ref[...])


def _proj(x, batch, seq, pos0, w, transposed):
    n, d = x.shape
    tm = min(PROJ_ROW_TILE, n)
    assert n % tm == 0
    lane = np.arange(LANES)
    inv = ROPE_THETA ** (-(lane % (HEAD_DIM // 2)).astype(np.float32) * 2.0 / HEAD_DIM)
    sign = np.where((lane % HEAD_DIM) < HEAD_DIM // 2, -1.0, 1.0).astype(np.float32)
    ang = (pos0 + jnp.arange(seq)).astype(F32)[:, None] * jnp.asarray(inv)[None, :]
    cos_t, sin_t = jnp.cos(ang), jnp.sin(ang) * jnp.asarray(sign)[None, :]
    if seq >= tm:
        assert seq % tm == 0
        per_seq = seq // tm
        tab_map = lambda i: (i % per_seq, 0)
    else:
        assert tm % seq == 0
        cos_t, sin_t = jnp.tile(cos_t, (tm // seq, 1)), jnp.tile(sin_t, (tm // seq, 1))
        tab_map = lambda i: (0, 0)

    row = lambda width: pl.BlockSpec((tm, width), lambda i: (i, 0))
    if transposed:
        per_seq = seq // tm
        kt_spec = pl.BlockSpec((1, MXU_DIM, tm), lambda i: (i // per_seq, 0, i % per_seq))
        kt_shape = jax.ShapeDtypeStruct((batch, MXU_DIM, seq), BF16)
    else:
        kt_spec = row(MXU_DIM)
        kt_shape = jax.ShapeDtypeStruct((n, MXU_DIM), BF16)

    weights = [w["wq"], w["wk"], w["wv"], w["wqi"], w["wki"], w["wwi"], w["wh"], w["wg"],
               w["bd"], w["pq"], w["qn"], w["kn"]]
    out_shape = [
        jax.ShapeDtypeStruct((n, N_HEADS * MXU_DIM), BF16),
        jax.ShapeDtypeStruct((n, KV_WIDTH), F32),
        jax.ShapeDtypeStruct((n, KV_WIDTH), F32),
        jax.ShapeDtypeStruct((n, KV_WIDTH), BF16),
        jax.ShapeDtypeStruct((n, IDX_HEADS * MXU_DIM), BF16),
        jax.ShapeDtypeStruct((n, IDX_DIM), F32),
        jax.ShapeDtypeStruct((n, LANES), F32),
        jax.ShapeDtypeStruct((n, 4 * HG_WIDTH), F32),
        jax.ShapeDtypeStruct((n, 2 * d), F32),
        kt_shape,
        kt_shape,
    ]
    out_specs = [row(N_HEADS * MXU_DIM), row(KV_WIDTH), row(KV_WIDTH), row(KV_WIDTH),
                 row(IDX_HEADS * MXU_DIM), row(IDX_DIM), row(LANES), row(4 * HG_WIDTH), row(2 * d),
                 kt_spec, kt_spec]
    return pl.pallas_call(
        functools.partial(_proj_kernel, transposed=transposed),
        grid=(n // tm,),
        in_specs=[row(d), _resident((1, d), 1)] + [_resident(a.shape, 1) for a in weights]
                 + [pl.BlockSpec((tm, LANES), tab_map), pl.BlockSpec((tm, LANES), tab_map)],
        out_specs=out_specs,
        out_shape=out_shape,
        compiler_params=_params(1),
        name="proj",
    )(x, w["mix_norm"], *weights, cos_t, sin_t)


def _hgrn_kernel(ph_ref, lbp_ref, s0_ref, gn_ref, tri_ref, rec_ref, s_ref, st_ref, o_ref,
                 *, layer, step, chunk, sub, t_valid):
    j = pl.program_id(1)

    @pl.when(j == 0)
    def _():
        for hh in range(HG_HEADS):
            st_ref[hh] = s0_ref[0, hh].T

    lbp = lbp_ref[...]
    e = jnp.exp(lbp - jnp.max(lbp, axis=0, keepdims=True))
    lb_all = jnp.sum(e[:layer + 1], axis=0, keepdims=True) / jnp.sum(e, axis=0, keepdims=True)

    tri = tri_ref[...]
    row_c = lax.broadcasted_iota(I32, (chunk, HG_KDIM), 0)
    row_s = lax.broadcasted_iota(I32, (sub, HG_KDIM), 0)
    causal = (lax.broadcasted_iota(I32, (chunk, chunk), 0) >= lax.broadcasted_iota(I32, (chunk, chunk), 1))
    w = HG_WIDTH

    def chunk_body(ci, carry):
        r0 = pl.multiple_of(ci * chunk, chunk)
        rows = pl.ds(r0, chunk)
        valid = (j * step + r0 + row_c) < t_valid
        heads = []
        worst = jnp.float32(0.0)
        for hh in range(HG_HEADS):
            lanes = slice(hh * HG_KDIM, (hh + 1) * HG_KDIM)
            hq = ph_ref[0, rows, hh * HG_KDIM:(hh + 1) * HG_KDIM]
            z = ph_ref[0, rows, w + hh * HG_KDIM:w + (hh + 1) * HG_KDIM]
            v = ph_ref[0, rows, 2 * w + hh * HG_VDIM:2 * w + (hh + 1) * HG_VDIM]
            hg = ph_ref[0, rows, 3 * w + hh * HG_VDIM:3 * w + (hh + 1) * HG_VDIM]
            lb = lb_all[:, lanes]
            q = hq * _sigmoid(hq) * (HG_KDIM ** -0.5)
            logf = jnp.where(valid, jnp.log(lb + (1.0 - lb) * _sigmoid(z)), 0.0)
            kk = jnp.where(valid, (1.0 - lb) * _sigmoid(-z), 0.0)
            l1 = logf.astype(BF16)
            r1 = logf - l1.astype(F32)
            l2 = r1.astype(BF16)
            l3 = (r1 - l2.astype(F32)).astype(BF16)
            g = _dot(tri, l1) + _dot(tri, l2) + _dot(tri, l3)
            heads.append((q, kk, v, hg, g))
            worst = jnp.maximum(worst, jnp.max(-g[chunk - 1:chunk, :]))

        def finish(hh, o, kk, v, hg, g):
            g_last = g[chunk - 1:chunk, :]
            k_end = (kk * jnp.exp(g_last - g)).astype(BF16)
            st_ref[hh] = jnp.exp(g_last) * st_ref[hh] + _dot_tn(v.astype(BF16), k_end)
            gate = hg * _sigmoid(hg)
            rec_ref[0, rows, hh * HG_VDIM:(hh + 1) * HG_VDIM] = (_rms(o, gn_ref[...]) * gate).astype(BF16)

        @pl.when(worst <= HG_SAFE_DECAY)
        def _():
            for hh, (q, kk, v, hg, g) in enumerate(heads):
                qg = (q * jnp.exp(g)).astype(BF16)
                a = jnp.where(causal, _dot_nt(qg, (kk * jnp.exp(-g)).astype(BF16)), 0.0)
                o = _dot_nt(qg, st_ref[hh].astype(BF16)) + _dot(a.astype(BF16), v.astype(BF16))
                finish(hh, o, kk, v, hg, g)

        @pl.when(worst > HG_SAFE_DECAY)
        def _():
            for hh, (q, kk, v, hg, g) in enumerate(heads):
                vb = v.astype(BF16)
                o_ref[hh] = _dot_nt((q * jnp.exp(g)).astype(BF16), st_ref[hh].astype(BF16))
                for i in range(chunk // sub):
                    b0 = i * sub
                    gi, qi, ki_, vi = g[b0:b0 + sub], q[b0:b0 + sub], kk[b0:b0 + sub], v[b0:b0 + sub]
                    if i > 0:
                        ref = g[b0 - 1:b0, :]
                        q_rel = (qi * jnp.exp(gi - ref)).astype(BF16)
                        k_rel = (kk[:b0] * jnp.exp(ref - g[:b0])).astype(BF16)
                        a = _dot_nt(q_rel, k_rel)
                        o_ref[hh, b0:b0 + sub, :] += _dot(a.astype(BF16), vb[:b0])
                    for t in range(sub):
                        dec = jnp.exp(jnp.where(row_s <= t, gi[t:t + 1, :] - gi, -jnp.inf))
                        a_t = jnp.sum(qi[t:t + 1, :] * dec * ki_, axis=1, keepdims=True)
                        o_ref[hh, b0 + t:b0 + t + 1, :] += jnp.sum(a_t * vi, axis=0, keepdims=True)
                finish(hh, o_ref[hh], kk, v, hg, g)
        return carry

    lax.fori_loop(0, step // chunk, chunk_body, 0)

    @pl.when(j == pl.num_programs(1) - 1)
    def _():
        for hh in range(HG_HEADS):
            s_ref[0, hh] = st_ref[hh].T


def _hgrn(ph, lb_param, s0, hg_norm, layer, t_valid):
    b, t, _ = ph.shape
    step = min(HG_STEP, t)
    chunk = min(HG_CHUNK, step)
    sub = min(HG_SUB, chunk)
    assert t % step == 0 and step % chunk == 0 and chunk % sub == 0
    tri = jnp.asarray(np.tril(np.ones((chunk, chunk), np.float32)), BF16)
    return pl.pallas_call(
        functools.partial(_hgrn_kernel, layer=layer, step=step, chunk=chunk, sub=sub, t_valid=t_valid),
        grid=(b, t // step),
        in_specs=[pl.BlockSpec((1, step, 4 * HG_WIDTH), lambda i, j: (i, j, 0)),
                  _resident(lb_param.shape, 2),
                  pl.BlockSpec((1, HG_HEADS, HG_KDIM, HG_VDIM), lambda i, j: (i, 0, 0, 0)),
                  _resident((1, HG_VDIM), 2), _resident(tri.shape, 2)],
        out_specs=[pl.BlockSpec((1, step, HG_WIDTH), lambda i, j: (i, j, 0)),
                   pl.BlockSpec((1, HG_HEADS, HG_KDIM, HG_VDIM), lambda i, j: (i, 0, 0, 0))],
        out_shape=[jax.ShapeDtypeStruct((b, t, HG_WIDTH), BF16),
                   jax.ShapeDtypeStruct((b, HG_HEADS, HG_KDIM, HG_VDIM), F32)],
        scratch_shapes=[pltpu.VMEM((HG_HEADS, HG_VDIM, HG_KDIM), F32),
                        pltpu.VMEM((HG_HEADS, chunk, HG_VDIM), F32)],
        compiler_params=_params(2),
        name="hgrn",
    )(ph, lb_param, s0, hg_norm.reshape(1, HG_VDIM), tri)


def _sortable_key(score):
    bits = pltpu.bitcast(score, I32)
    return jnp.where(bits < 0, INT_MIN - bits, bits)


def _kth_largest(count_ge, topk, shape):
    zero = jnp.zeros(shape, I32)
    c = count_ge(zero)
    ok = c >= topk
    tau = jnp.where(ok, zero, jnp.full(shape, INT_MIN, I32))
    n_ge = jnp.where(ok, c, 0)

    def step(it, carry):
        tau, n_ge = carry
        cand = tau | lax.shift_left(jnp.int32(1), 30 - it)
        c = count_ge(cand)
        ok = c >= topk
        return jnp.where(ok, cand, tau), jnp.where(ok, c, n_ge)

    return lax.fori_loop(0, 31, step, (tau, n_ge))


def _dsa_prompt_kernel(qx_ref, qc_ref, wi_ref, kct_ref, kt_ref, v_ref, tri_ref, sel_ref, o_ref,
                       keys_ref, keyst_ref, aq_ref, ac_ref, wb_ref, m_ref, l_ref, acc_ref, lg_ref, p_ref, bias_ref,
                       *, tq, ck, topk):
    i = pl.program_id(1)
    row0 = i * tq
    n_chunks = (row0 + tq + ck - 1) // ck
    n_sub = ck // LANES
    rows_q = N_HEADS * tq

    for h in range(IDX_HEADS):
        ac_ref[h * tq:(h + 1) * tq, :] = qc_ref[0, :, h * MXU_DIM:(h + 1) * MXU_DIM]
        aq_ref[h * tq:(h + 1) * tq, :] = qx_ref[0, :, h * MXU_DIM:(h + 1) * MXU_DIM]
        wb_ref[h] = jnp.broadcast_to(wi_ref[0, :, h:h + 1], (tq, LANES))

    row_id = row0 + lax.broadcasted_iota(I32, (tq, ck), 0)
    col_id = lax.broadcasted_iota(I32, (tq, ck), 1)

    def score_body(c, carry):
        col0 = pl.multiple_of(c * ck, ck)
        s = _dot(ac_ref[...], kct_ref[0, :, pl.ds(col0, ck)])
        tot = jnp.zeros((tq, ck), F32)
        for h in range(IDX_HEADS):
            tot = tot + jnp.maximum(s[h * tq:(h + 1) * tq, :], 0.0) * jnp.tile(wb_ref[h], (1, n_sub))
        key = jnp.where(col0 + col_id <= row_id, _sortable_key(tot), INT_MIN)
        keys_ref[:, pl.ds(col0, ck)] = key
        keyst_ref[pl.ds(col0, ck), :] = key.T
        return carry

    lax.fori_loop(0, n_chunks, score_body, 0)

    def count(cmp, cand):
        def body(c, cnt):
            r0 = pl.multiple_of(c * ck, ck)
            hit = jnp.where(cmp(keyst_ref[pl.ds(r0, ck), :], cand), 1, 0)
            return cnt + jnp.sum(hit.reshape(ck // COUNT_ACC, COUNT_ACC, tq), axis=0)
        cnt = lax.fori_loop(0, n_chunks, body, jnp.zeros((COUNT_ACC, tq), I32))
        return jnp.sum(cnt, axis=0, keepdims=True)

    def per_row(x):
        return jnp.broadcast_to(x, (LANES, tq)).T

    tau, n_ge = _kth_largest(lambda cand: count(lambda a, b: a >= b, cand), topk, (1, tq))
    tau_lanes = jnp.maximum(tau, INT_MIN + 1)
    tau_sel = per_row(tau_lanes)
    has_ties = jnp.max(n_ge) > topk

    m_ref[...] = jnp.full(m_ref.shape, MASKED_MAX, F32)
    l_ref[...] = jnp.zeros(l_ref.shape, F32)
    acc_ref[...] = jnp.zeros(acc_ref.shape, F32)

    def attend(ranked):
        tau_t = jnp.tile(tau_sel, (1, n_sub))
        if ranked:
            need_eq = per_row((topk - count(lambda a, b: a > b, tau_lanes)).astype(F32))[:, 0:1]

        def body(c, eq_seen):
            col0 = pl.multiple_of(c * ck, ck)
            blk = keys_ref[:, pl.ds(col0, ck)]
            if ranked:
                eq = blk == tau_t
                eq_f = jnp.where(eq, 1.0, 0.0)
                before = _dot(eq_f.astype(BF16), tri_ref[...]) + eq_seen
                rank = jnp.where(eq, before, jnp.where(blk > tau_t, -1.0, jnp.inf))
                bias_ref[...] = jnp.where(rank < need_eq, 0.0, -jnp.inf)
                eq_seen = eq_seen + jnp.sum(eq_f, axis=1, keepdims=True)
            else:
                bias_ref[...] = jnp.where(blk >= tau_t, 0.0, -jnp.inf)
            for g in range(rows_q // ATT_ROWS):
                rows = slice(g * ATT_ROWS, (g + 1) * ATT_ROWS)
                off = (g * ATT_ROWS) % tq
                lg_ref[rows, :] = (_dot(aq_ref[rows, :], kt_ref[0, :, pl.ds(col0, ck)])
                                   + bias_ref[off:off + ATT_ROWS, :])
                m_old = m_ref[rows, :]
                m_new = jnp.maximum(m_old, jnp.max(lg_ref[rows, :], axis=1, keepdims=True))
                m_ref[rows, :] = m_new
                alpha = jnp.exp2(m_old - m_new)
                p = jnp.exp2(lg_ref[rows, :] - jnp.tile(m_new, (1, n_sub)))
                p_ref[rows, :] = p.astype(BF16)
                l_ref[rows, :] = alpha * l_ref[rows, :] + jnp.sum(p, axis=1, keepdims=True)
                acc_ref[rows, :] = (jnp.tile(alpha, (1, KV_WIDTH // LANES)) * acc_ref[rows, :]
                                    + _dot(p_ref[rows, :], v_ref[0, pl.ds(col0, ck), :]))
            return eq_seen
        lax.fori_loop(0, n_chunks, body, jnp.zeros((tq, 1), F32))

    @pl.when(has_ties)
    def _():
        attend(True)

    @pl.when(jnp.logical_not(has_ties))
    def _():
        attend(False)

    out = jnp.zeros((tq, ATT_WIDTH), F32)
    for h in range(N_HEADS):
        rows = slice(h * tq, (h + 1) * tq)
        o_h = acc_ref[rows, :] / jnp.tile(l_ref[rows, :], (1, KV_WIDTH // LANES))
        out = out + _dot(o_h.astype(BF16), sel_ref[h])
    o_ref[0] = out.astype(BF16)


def _dsa_prompt(qx, qc, wi, kct, kt, vb, sel):
    b, t, _ = qx.shape
    tq, ck = min(Q_TILE, t), min(K_CHUNK, t)
    assert t % tq == 0 and t % ck == 0 and ck % tq == 0
    topk = min(TOPK_MAX, t // 4)
    tri = jnp.asarray(np.triu(np.ones((ck, ck), np.float32), 1), BF16)
    tile = lambda width: pl.BlockSpec((1, tq, width), lambda i, j: (i, j, 0))
    whole = lambda rows, cols: pl.BlockSpec((1, rows, cols), lambda i, j: (i, 0, 0),
                                            pipeline_mode=pl.Buffered(1))
    return pl.pallas_call(
        functools.partial(_dsa_prompt_kernel, tq=tq, ck=ck, topk=topk),
        grid=(b, t // tq),
        in_specs=[tile(N_HEADS * MXU_DIM), tile(IDX_HEADS * MXU_DIM), tile(LANES),
                  whole(MXU_DIM, t), whole(MXU_DIM, t), whole(t, KV_WIDTH),
                  _resident(tri.shape, 2), _resident(sel.shape, 2)],
        out_specs=tile(ATT_WIDTH),
        out_shape=jax.ShapeDtypeStruct((b, t, ATT_WIDTH), BF16),
        scratch_shapes=[pltpu.VMEM((tq, t), I32),
                        pltpu.VMEM((t, tq), I32),
                        pltpu.VMEM((N_HEADS * tq, MXU_DIM), BF16),
                        pltpu.VMEM((IDX_HEADS * tq, MXU_DIM), BF16),
                        pltpu.VMEM((IDX_HEADS, tq, LANES), F32),
                        pltpu.VMEM((N_HEADS * tq, LANES), F32),
                        pltpu.VMEM((N_HEADS * tq, LANES), F32),
                        pltpu.VMEM((N_HEADS * tq, KV_WIDTH), F32),
                        pltpu.VMEM((N_HEADS * tq, ck), F32),
                        pltpu.VMEM((N_HEADS * tq, ck), BF16),
                        pltpu.VMEM((tq, ck), F32)],
        compiler_params=_params(2),
        name="dsa_prompt",
    )(qx, qc, wi, kct, kt, vb, tri, sel)


def _dsa_sample_kernel(pt_ref, qx_ref, qc_ref, wb_ref, kcn_ref, kn_ref, vn_ref, tri_ref,
                       cik_hbm, ck_hbm, cv_hbm, o_ref,
                       ik_buf, k_buf, v_buf, keys_ref, bias_ref, lg_ref, sem,
                       *, layer, n_pages, page, chunk, tn, topk):
    b = pl.program_id(0)
    nb = pl.num_programs(0)
    past = n_pages * page
    width = past + LANES
    rows = tn * N_HEADS

    def copies(sample, slot, p):
        pg = pt_ref[sample, p]
        dst = pl.ds(pl.multiple_of(p * page, page), page)
        return (pltpu.make_async_copy(cik_hbm.at[layer, pg], ik_buf.at[slot, :, dst], sem.at[slot, 0]),
                pltpu.make_async_copy(ck_hbm.at[layer, pg], k_buf.at[slot, :, dst], sem.at[slot, 1]),
                pltpu.make_async_copy(cv_hbm.at[layer, pg], v_buf.at[slot, :, dst], sem.at[slot, 2]))

    def fetch(sample, slot):
        def body(p, carry):
            for cp in copies(sample, slot, p):
                cp.start()
            return carry
        lax.fori_loop(0, n_pages, body, 0)

    def wait(sample, slot):
        def body(p, carry):
            for cp in copies(sample, slot, p):
                cp.wait()
            return carry
        lax.fori_loop(0, n_pages, body, 0)

    slot = b % 2

    @pl.when(b == 0)
    def _():
        fetch(0, 0)

    @pl.when(b + 1 < nb)
    def _():
        fetch(b + 1, 1 - slot)

    wait(b, slot)

    qc = qc_ref[0]
    qx = qx_ref[0]
    wb = wb_ref[0]
    tok = lax.broadcasted_iota(I32, (tn, LANES), 0)
    col = lax.broadcasted_iota(I32, (tn, LANES), 1)

    def head_sum(s):
        n = s.shape[1]
        r = jnp.maximum(s, 0.0) * jnp.tile(wb, (1, n // LANES))
        return jnp.sum(r.reshape(tn, N_HEADS, n), axis=1)

    keys_ref[...] = jnp.full(keys_ref.shape, INT_MIN, I32)

    zero_rows = jnp.zeros((MXU_DIM - 3 * IDX_DIM, chunk), BF16)
    for c in range(past // chunk):
        cols = slice(c * chunk, (c + 1) * chunk)
        hi, lo = _split2(ik_buf[slot, :, cols])
        kct = jnp.concatenate([hi, hi, lo, zero_rows], axis=0)
        keys_ref[0:tn, cols] = _sortable_key(head_sum(_dot(qc, kct)))
    s_new = head_sum(_dot(qc, kcn_ref[0]))
    keys_ref[0:tn, past:width] = jnp.where(col <= tok, _sortable_key(s_new), INT_MIN)

    def count_ge(cand):
        hit = jnp.where(keys_ref[...] >= cand, 1, 0)
        parts = [hit[:, u * LANES:(u + 1) * LANES] for u in range(width // LANES)]
        while len(parts) > 1:
            parts = [a + b for a, b in zip(parts[0::2], parts[1::2])] + parts[len(parts) & ~1:]
        return jnp.sum(parts[0], axis=1, keepdims=True)

    tau, n_ge = _kth_largest(count_ge, topk, (8, 1))
    tau_sel = jnp.maximum(tau, INT_MIN + 1)
    has_ties = jnp.max(n_ge) > topk

    @pl.when(jnp.logical_not(has_ties))
    def _():
        bias_ref[...] = jnp.where(keys_ref[...] >= tau_sel, 0.0, -jnp.inf)

    @pl.when(has_ties)
    def _():
        n_gt = jnp.sum(jnp.where(keys_ref[...] > tau_sel, 1, 0), axis=1, keepdims=True)
        need_eq = (topk - n_gt).astype(F32)

        def body(u, eq_seen):
            c0 = pl.multiple_of(u * LANES, LANES)
            blk = keys_ref[:, pl.ds(c0, LANES)]
            eq = blk == tau_sel
            eq_f = jnp.where(eq, 1.0, 0.0)
            before = _dot(eq_f.astype(BF16), tri_ref[...]) + eq_seen
            rank = jnp.where(eq, before, jnp.where(blk > tau_sel, -1.0, jnp.inf))
            bias_ref[:, pl.ds(c0, LANES)] = jnp.where(rank < need_eq, 0.0, -jnp.inf)
            return eq_seen + jnp.sum(eq_f, axis=1, keepdims=True)

        lax.fori_loop(0, width // LANES, body, jnp.zeros((8, 1), F32))

    def expand(x):
        n = x.shape[1]
        return jnp.broadcast_to(x[0:tn].reshape(tn, 1, n), (tn, N_HEADS, n)).reshape(rows, n)

    for c in range(past // chunk):
        cols = slice(c * chunk, (c + 1) * chunk)
        lg_ref[:, cols] = _dot(qx, k_buf[slot, :, cols].astype(BF16)) + expand(bias_ref[:, cols])
    lg_ref[:, past:width] = _dot(qx, kn_ref[0]) + expand(bias_ref[:, past:width])
    lg = lg_ref[...]
    p = jnp.exp2(lg - jnp.max(lg, axis=1, keepdims=True))
    l = jnp.sum(p, axis=1, keepdims=True)
    lg_ref[...] = p
    acc = _dot_nt(lg_ref[:, past:width].astype(BF16), vn_ref[0])
    for c in range(past // chunk):
        cols = slice(c * chunk, (c + 1) * chunk)
        acc = acc + _dot_nt(lg_ref[:, cols].astype(BF16), v_buf[slot, :, cols].astype(BF16))
    o_ref[0] = acc / l


def _dsa_sample(page_table, qx, qc, wb, kcn, kn, vn, cache_ik, cache_k, cache_v, layer):
    db, rows, _ = qx.shape
    tn = rows // N_HEADS
    n_pages = page_table.shape[1]
    page = cache_ik.shape[3]
    past = n_pages * page
    chunk = min(SAMPLE_CHUNK, past)
    assert page == LANES and tn <= 8 and past % chunk == 0
    tri = jnp.asarray(np.triu(np.ones((page, page), np.float32), 1), BF16)
    topk = min(TOPK_MAX, (past + tn) // 4)
    per = lambda r, c: pl.BlockSpec((1, r, c), lambda i, pt: (i, 0, 0))
    grid_spec = pltpu.PrefetchScalarGridSpec(
        num_scalar_prefetch=1,
        grid=(db,),
        in_specs=[per(rows, MXU_DIM), per(rows, MXU_DIM), per(rows, LANES),
                  per(MXU_DIM, LANES), per(KV_WIDTH, LANES), per(KV_WIDTH, LANES),
                  pl.BlockSpec(tri.shape, lambda i, pt: (0, 0)),
                  pl.BlockSpec(memory_space=pl.ANY), pl.BlockSpec(memory_space=pl.ANY),
                  pl.BlockSpec(memory_space=pl.ANY)],
        out_specs=per(rows, KV_WIDTH),
        scratch_shapes=[pltpu.VMEM((2, IDX_DIM, past), F32),
                        pltpu.VMEM((2, KV_WIDTH, past), F32),
                        pltpu.VMEM((2, KV_WIDTH, past), F32),
                        pltpu.VMEM((8, past + LANES), I32),
                        pltpu.VMEM((8, past + LANES), F32),
                        pltpu.VMEM((rows, past + LANES), F32),
                        pltpu.SemaphoreType.DMA((2, 3))],
    )
    return pl.pallas_call(
        functools.partial(_dsa_sample_kernel, layer=layer, n_pages=n_pages, page=page, chunk=chunk,
                          tn=tn, topk=topk),
        grid_spec=grid_spec,
        out_shape=jax.ShapeDtypeStruct((db, rows, KV_WIDTH), F32),
        compiler_params=_params(1),
        name="dsa_sample",
    )(page_table, qx, qc, wb, kcn, kn, vn, tri, cache_ik, cache_k, cache_v)


def _merge_ffn_kernel(x_ref, att_ref, rec_ref, pg_ref, wa_ref, wh_ref, wo_ref,
                      g_ref, wg_ref, wu_ref, wd_ref, o_ref):
    d = x_ref.shape[1]
    pg = pg_ref[...]
    merged = (_sigmoid(pg[:, :d]) * _dot(att_ref[...], wa_ref[...])
              + _sigmoid(pg[:, d:]) * _dot(rec_ref[...], wh_ref[...]))
    x = x_ref[...] + _dot(merged.astype(BF16), wo_ref[...])
    o_ref[...] = _ffn_apply(x, g_ref, wg_ref, wu_ref, wd_ref)


def _merge_ffn(x, att, rec, pg, wa, wh, wo, norm, wg, wu, wd):
    n, d = x.shape
    tm = min(ROW_TILE, n)
    assert n % tm == 0 and wg.shape[1] % FFN_CHUNK == 0
    row = lambda width: pl.BlockSpec((tm, width), lambda i: (i, 0))
    weights = [wa, wh, wo, norm.reshape(1, d), wg, wu, wd]
    return pl.pallas_call(
        _merge_ffn_kernel,
        grid=(n // tm,),
        in_specs=[row(d), row(ATT_WIDTH), row(HG_WIDTH), row(2 * d)] + [_resident(a.shape, 1) for a in weights],
        out_specs=row(d),
        out_shape=jax.ShapeDtypeStruct((n, d), F32),
        compiler_params=_params(1),
        name="merge_ffn",
    )(x, att, rec, pg, *weights)


def _prep_layer(l, ffn1_norm, ffn1_w_gate, ffn1_w_up, ffn1_w_down, mix_norm, w_in, q_norm, k_norm, hg_norm,
                w_branch_attn, w_branch_hgrn, w_out, ffn2_norm, ffn2_w_gate, ffn2_w_up, ffn2_w_down):
    d = w_in.shape[1]
    w = w_in[l]
    off = [0]

    def take(width):
        a = w[:, off[0]:off[0] + width]
        off[0] += width
        return a

    wq, wk, wv = take(ATT_WIDTH), take(KV_WIDTH), take(KV_WIDTH)
    wqi, wki, wwi = take(IDX_HEADS * IDX_DIM), take(IDX_DIM), take(IDX_HEADS)
    wh = take(4 * HG_WIDTH)
    wg = take(2 * d)
    zeros = jnp.zeros((d, IDX_DIM), F32)
    wqi_heads = wqi.reshape(d, IDX_HEADS, IDX_DIM)
    wqi_arr = jnp.concatenate([wqi_heads, wqi_heads, wqi_heads, jnp.zeros_like(wqi_heads)], axis=2)
    wki_arr = jnp.concatenate([wki, wki, wki, zeros], axis=1)
    wwi_arr = jnp.concatenate([wwi, jnp.zeros((d, LANES - IDX_HEADS), F32)], axis=1)
    bd = np.kron(np.eye(ATT_WIDTH // HEAD_DIM, dtype=np.float32), np.ones((HEAD_DIM, HEAD_DIM), np.float32))
    pq = np.zeros((ATT_WIDTH, N_HEADS * MXU_DIM), np.float32)
    sel = np.zeros((N_HEADS, KV_WIDTH, ATT_WIDTH), np.float32)
    for h in range(N_HEADS):
        g = h // (N_HEADS // KV_HEADS)
        for e in range(HEAD_DIM):
            pq[h * HEAD_DIM + e, h * MXU_DIM + g * HEAD_DIM + e] = 1.0
            sel[h, g * HEAD_DIM + e, h * HEAD_DIM + e] = 1.0
    bf = lambda a: a.astype(BF16)
    return dict(
        ffn1=(ffn1_norm[l], bf(ffn1_w_gate[l]), bf(ffn1_w_up[l]), bf(ffn1_w_down[l])),
        ffn2=(ffn2_norm[l], bf(ffn2_w_gate[l]), bf(ffn2_w_up[l]), bf(ffn2_w_down[l])),
        proj=dict(mix_norm=mix_norm[l].reshape(1, d), wq=bf(wq), wk=bf(wk), wv=bf(wv),
                  wqi=bf(wqi_arr.reshape(d, IDX_HEADS * MXU_DIM)), wki=bf(wki_arr), wwi=bf(wwi_arr),
                  wh=bf(wh), wg=bf(wg), bd=jnp.asarray(bd, BF16), pq=jnp.asarray(pq, BF16),
                  qn=jnp.tile(q_norm[l], N_HEADS).reshape(1, ATT_WIDTH),
                  kn=jnp.tile(k_norm[l], KV_HEADS).reshape(1, KV_WIDTH)),
        sel=jnp.asarray(sel, BF16),
        hg_norm=hg_norm[l],
        merge=(bf(w_branch_attn[l]), bf(w_branch_hgrn[l]), bf(w_out[l])),
    )


def kernel(x_prompt, x_sample, cache_k, cache_v, cache_idx_k, state_hgrn, page_table, hg_lower_bound,
           ffn1_norm, ffn1_w_gate, ffn1_w_up, ffn1_w_down, mix_norm, w_in, q_norm, k_norm, hg_norm,
           w_branch_attn, w_branch_hgrn, w_out, ffn2_norm, ffn2_w_gate, ffn2_w_up, ffn2_w_down):
    b, t, d = x_prompt.shape
    db, tn, _ = x_sample.shape
    depth = w_in.shape[0]
    n_pool, page = cache_k.shape[1], cache_k.shape[2]
    past = page_table.shape[1] * page
    cache_ikt = cache_idx_k.transpose(0, 1, 3, 2)
    cache_kt = cache_k.transpose(0, 1, 3, 4, 2).reshape(depth, n_pool, KV_WIDTH, page)
    cache_vt = cache_v.transpose(0, 1, 3, 4, 2).reshape(depth, n_pool, KV_WIDTH, page)
    lb_param = hg_lower_bound.astype(F32)
    tn_pad = -(-tn // HG_SUB) * HG_SUB

    xp = x_prompt.reshape(b * t, d)
    xs = x_sample.reshape(db * tn, d)
    outs = [[] for _ in range(8)]
    for l in range(depth):
        w = _prep_layer(l, ffn1_norm, ffn1_w_gate, ffn1_w_up, ffn1_w_down, mix_norm, w_in, q_norm, k_norm,
                        hg_norm, w_branch_attn, w_branch_hgrn, w_out, ffn2_norm, ffn2_w_gate, ffn2_w_up,
                        ffn2_w_down)
        xp = _ffn(xp, *w["ffn1"])
        qx, k, v, vb, qc, ki, wi, ph, pg, kt, kct = _proj(xp, b, t, 0, w["proj"], transposed=True)
        att = _dsa_prompt(qx.reshape(b, t, -1), qc.reshape(b, t, -1), wi.reshape(b, t, -1),
                          kct, kt, vb.reshape(b, t, -1), w["sel"])
        rec, s_p = _hgrn(ph.reshape(b, t, -1), lb_param, jnp.zeros((b, HG_HEADS, HG_KDIM, HG_VDIM), F32),
                         w["hg_norm"], l, t)
        xp = _merge_ffn(xp, att.reshape(b * t, -1), rec.reshape(b * t, -1), pg, *w["merge"], *w["ffn2"])
        outs[0].append(k.reshape(b, t, KV_HEADS, HEAD_DIM))
        outs[1].append(v.reshape(b, t, KV_HEADS, HEAD_DIM))
        outs[2].append(ki.reshape(b, t, IDX_DIM))
        outs[3].append(s_p.astype(x_prompt.dtype))

        xs = _ffn(xs, *w["ffn1"])
        qx, k, v, vb, qc, ki, wi, ph, pg, kb, kc = _proj(xs, db, tn, past, w["proj"], transposed=False)
        rows = tn * N_HEADS
        qx3 = qx.reshape(db, rows, MXU_DIM)
        qc3 = qc.reshape(db, rows, MXU_DIM)
        wb3 = jnp.broadcast_to(wi[:, :IDX_HEADS].reshape(db, rows, 1), (db, rows, LANES))
        pad_new = lambda a: jnp.pad(a.reshape(db, tn, -1), ((0, 0), (0, page - tn), (0, 0))).transpose(0, 2, 1)
        o_s = _dsa_sample(page_table, qx3, qc3, wb3, pad_new(kc), pad_new(kb), pad_new(vb),
                          cache_ikt, cache_kt, cache_vt, l)
        o_s = o_s.reshape(db, tn, KV_HEADS, N_HEADS // KV_HEADS, KV_HEADS, HEAD_DIM)
        att_s = jnp.stack([o_s[:, :, g, :, g, :] for g in range(KV_HEADS)], axis=2)
        att_s = att_s.reshape(db * tn, ATT_WIDTH).astype(BF16)
        ph3 = jnp.pad(ph.reshape(db, tn, -1), ((0, 0), (0, tn_pad - tn), (0, 0)))
        rec, s_s = _hgrn(ph3, lb_param, state_hgrn[l], w["hg_norm"], l, tn)
        rec = rec[:, :tn].reshape(db * tn, HG_WIDTH)
        xs = _merge_ffn(xs, att_s, rec, pg, *w["merge"], *w["ffn2"])
        outs[4].append(k.reshape(db, tn, KV_HEADS, HEAD_DIM))
        outs[5].append(v.reshape(db, tn, KV_HEADS, HEAD_DIM))
        outs[6].append(ki.reshape(db, tn, IDX_DIM))
        outs[7].append(s_s.astype(state_hgrn.dtype))

    return (xp.reshape(b, t, d), xs.reshape(db, tn, d)) + tuple(jnp.stack(o) for o in outs)
```

```python
import functools

import numpy as np
import jax
import jax.numpy as jnp
from jax import lax
from jax.experimental import pallas as pl
from jax.experimental.pallas import tpu as pltpu

F32, BF16, I32 = jnp.float32, jnp.bfloat16, jnp.int32

N_HEADS, KV_HEADS, HEAD_DIM = 8, 4, 64
IDX_HEADS, IDX_DIM, TOPK_MAX = 8, 64, 256
HG_HEADS, HG_KDIM, HG_VDIM = 4, 128, 128
ATT_WIDTH = N_HEADS * HEAD_DIM
KV_WIDTH = KV_HEADS * HEAD_DIM
HG_WIDTH = HG_HEADS * HG_KDIM
ROPE_THETA = 10000.0
NORM_EPS = 1e-6

LANES = 128
MXU_DIM = 256
VMEM_LIMIT_BYTES = 56 * 2**20
INT_MIN = -2**31
MASKED_MAX = -1e30
LOG2E = 1.4426950408889634

ROW_TILE = 512
PROJ_ROW_TILE = 256
FFN_CHUNK = 256
HG_STEP = 256
HG_CHUNK = 64
HG_SAFE_DECAY = 80.0
HG_SUB = 16
Q_TILE = 256
K_CHUNK = 512
ATT_ROWS = 256
PLANE_KEYS = 256
SAMPLE_CHUNK = 2048


def _params(n_axes):
    return pltpu.CompilerParams(dimension_semantics=("arbitrary",) * n_axes,
                                vmem_limit_bytes=VMEM_LIMIT_BYTES)


def _resident(shape, n_axes):
    zeros = (0,) * len(shape)
    if n_axes == 1:
        return pl.BlockSpec(shape, lambda i: zeros, pipeline_mode=pl.Buffered(1))
    return pl.BlockSpec(shape, lambda i, j: zeros, pipeline_mode=pl.Buffered(1))


def _rms(x, gain):
    return x * lax.rsqrt(jnp.mean(x * x, axis=-1, keepdims=True) + NORM_EPS) * gain


def _sigmoid(x):
    return 1.0 / (1.0 + jnp.exp(-x))


def _dot(a, b):
    return jnp.dot(a, b, preferred_element_type=F32)


def _dot_nt(a, b):
    return lax.dot_general(a, b, (((1,), (1,)), ((), ())), preferred_element_type=F32)


def _dot_tn(a, b):
    return lax.dot_general(a, b, (((0,), (0,)), ((), ())), preferred_element_type=F32)


def _split2(x):
    hi = x.astype(BF16)
    return hi, (x - hi.astype(F32)).astype(BF16)


def _ffn_apply(x, g_ref, wg_ref, wu_ref, wd_ref):
    h = _rms(x, g_ref[...]).astype(BF16)
    d_ff = wg_ref.shape[1]
    acc = jnp.zeros(x.shape, F32)
    for c in range(d_ff // FFN_CHUNK):
        cols = slice(c * FFN_CHUNK, (c + 1) * FFN_CHUNK)
        g = _dot(h, wg_ref[:, cols])
        u = _dot(h, wu_ref[:, cols])
        a = (g * _sigmoid(g) * u).astype(BF16)
        acc = acc + _dot(a, wd_ref[cols, :])
    return x + 0.5 * acc


def _ffn_kernel(x_ref, g_ref, wg_ref, wu_ref, wd_ref, o_ref):
    o_ref[...] = _ffn_apply(x_ref[...], g_ref, wg_ref, wu_ref, wd_ref)


def _ffn(x, norm, wg, wu, wd):
    n, d = x.shape
    tm = min(ROW_TILE, n)
    assert n % tm == 0 and wg.shape[1] % FFN_CHUNK == 0
    return pl.pallas_call(
        _ffn_kernel,
        grid=(n // tm,),
        in_specs=[pl.BlockSpec((tm, d), lambda i: (i, 0)), _resident((1, d), 1),
                  _resident(wg.shape, 1), _resident(wu.shape, 1), _resident(wd.shape, 1)],
        out_specs=pl.BlockSpec((tm, d), lambda i: (i, 0)),
        out_shape=jax.ShapeDtypeStruct((n, d), F32),
        compiler_params=_params(1),
        name="ffn",
    )(x, norm.reshape(1, d), wg, wu, wd)


def _rope(y, cos, sin):
    n = y.shape[1]
    reps = n // LANES
    c = jnp.tile(cos, (1, reps)) if reps > 1 else cos
    s = jnp.tile(sin, (1, reps)) if reps > 1 else sin
    lane = lax.broadcasted_iota(I32, y.shape, 1)
    first_half = (lane & (HEAD_DIM // 2)) == 0
    partner = jnp.where(first_half, pltpu.roll(y, n - HEAD_DIM // 2, axis=1),
                        pltpu.roll(y, HEAD_DIM // 2, axis=1))
    return y * c + partner * s


def _proj_kernel(x_ref, gn_ref, wq_ref, wk_ref, wv_ref, wqi_ref, wki_ref, wwi_ref, wh_ref, wg_ref,
                 bd_ref, pq_ref, qn_ref, kn_ref, cos_ref, sin_ref,
                 qx_ref, k_ref, v_ref, vb_ref, qc_ref, ki_ref, wi_ref, ph_ref, pg_ref, kb_ref, kc_ref,
                 *, transposed):
    h = _rms(x_ref[...], gn_ref[...]).astype(BF16)
    cos, sin = cos_ref[...], sin_ref[...]

    def head_norm(y, gain):
        n = y.shape[1]
        hi, lo = _split2(y * y)
        bd = bd_ref[:n, :n]
        ss = _dot(hi, bd) + _dot(lo, bd)
        return y * lax.rsqrt(ss * (1.0 / HEAD_DIM) + NORM_EPS) * gain

    q = _rope(head_norm(_dot(h, wq_ref[...]), qn_ref[...]), cos, sin) * (LOG2E * HEAD_DIM ** -0.5)
    qx_ref[...] = _dot(q.astype(BF16), pq_ref[...]).astype(BF16)

    k = _rope(head_norm(_dot(h, wk_ref[...]), kn_ref[...]), cos, sin)
    k_ref[...] = k
    v = _dot(h, wv_ref[...])
    v_ref[...] = v
    vb_ref[...] = v.astype(BF16)

    qi = _rope(_dot(h, wqi_ref[...]), cos, sin) * (IDX_DIM ** -0.5)
    lane = lax.broadcasted_iota(I32, qi.shape, 1) & (MXU_DIM - 1)
    is_lo = (lane >= IDX_DIM) & (lane < 2 * IDX_DIM)
    qc_ref[...] = jnp.where(is_lo, qi - qi.astype(BF16).astype(F32), qi).astype(BF16)

    ki = _rope(_dot(h, wki_ref[...]), cos, sin)
    ki_ref[...] = ki[:, :IDX_DIM]
    lane = lax.broadcasted_iota(I32, ki.shape, 1)
    is_lo = (lane >= 2 * IDX_DIM) & (lane < 3 * IDX_DIM)
    kc = jnp.where(is_lo, ki - ki.astype(BF16).astype(F32), ki)

    if transposed:
        kb_ref[0] = k.T.astype(BF16)
        kc_ref[0] = kc.T.astype(BF16)
    else:
        kb_ref[...] = k.astype(BF16)
        kc_ref[...] = kc.astype(BF16)

    wi_ref[...] = _dot(h, wwi_ref[...]) * (IDX_HEADS ** -0.5)
    ph_ref[...] = _dot(h, wh_ref[...])
    pg_ref[...] = _dot(h, wg_ref[...])


def _proj(x, batch, seq, pos0, w, transposed):
    n, d = x.shape
    tm = min(PROJ_ROW_TILE, n)
    assert n % tm == 0
    lane = np.arange(LANES)
    inv = ROPE_THETA ** (-(lane % (HEAD_DIM // 2)).astype(np.float32) * 2.0 / HEAD_DIM)
    sign = np.where((lane % HEAD_DIM) < HEAD_DIM // 2, -1.0, 1.0).astype(np.float32)
    ang = (pos0 + jnp.arange(seq)).astype(F32)[:, None] * jnp.asarray(inv)[None, :]
    cos_t, sin_t = jnp.cos(ang), jnp.sin(ang) * jnp.asarray(sign)[None, :]
    if seq >= tm:
        assert seq % tm == 0
        per_seq = seq // tm
        tab_map = lambda i: (i % per_seq, 0)
    else:
        assert tm % seq == 0
        cos_t, sin_t = jnp.tile(cos_t, (tm // seq, 1)), jnp.tile(sin_t, (tm // seq, 1))
        tab_map = lambda i: (0, 0)

    row = lambda width: pl.BlockSpec((tm, width), lambda i: (i, 0))
    if transposed:
        per_seq = seq // tm
        kt_spec = pl.BlockSpec((1, MXU_DIM, tm), lambda i: (i // per_seq, 0, i % per_seq))
        kt_shape = jax.ShapeDtypeStruct((batch, MXU_DIM, seq), BF16)
    else:
        kt_spec = row(MXU_DIM)
        kt_shape = jax.ShapeDtypeStruct((n, MXU_DIM), BF16)

    weights = [w["wq"], w["wk"], w["wv"], w["wqi"], w["wki"], w["wwi"], w["wh"], w["wg"],
               w["bd"], w["pq"], w["qn"], w["kn"]]
    out_shape = [
        jax.ShapeDtypeStruct((n, N_HEADS * MXU_DIM), BF16),
        jax.ShapeDtypeStruct((n, KV_WIDTH), F32),
        jax.ShapeDtypeStruct((n, KV_WIDTH), F32),
        jax.ShapeDtypeStruct((n, KV_WIDTH), BF16),
        jax.ShapeDtypeStruct((n, IDX_HEADS * MXU_DIM), BF16),
        jax.ShapeDtypeStruct((n, IDX_DIM), F32),
        jax.ShapeDtypeStruct((n, LANES), F32),
        jax.ShapeDtypeStruct((n, 4 * HG_WIDTH), F32),
        jax.ShapeDtypeStruct((n, 2 * d), F32),
        kt_shape,
        kt_shape,
    ]
    out_specs = [row(N_HEADS * MXU_DIM), row(KV_WIDTH), row(KV_WIDTH), row(KV_WIDTH),
                 row(IDX_HEADS * MXU_DIM), row(IDX_DIM), row(LANES), row(4 * HG_WIDTH), row(2 * d),
                 kt_spec, kt_spec]
    return pl.pallas_call(
        functools.partial(_proj_kernel, transposed=transposed),
        grid=(n // tm,),
        in_specs=[row(d), _resident((1, d), 1)] + [_resident(a.shape, 1) for a in weights]
                 + [pl.BlockSpec((tm, LANES), tab_map), pl.BlockSpec((tm, LANES), tab_map)],
        out_specs=out_specs,
        out_shape=out_shape,
        compiler_params=_params(1),
        name="proj",
    )(x, w["mix_norm"], *weights, cos_t, sin_t)


def _hgrn_kernel(ph_ref, lbp_ref, s0_ref, gn_ref, tri_ref, rec_ref, s_ref, st_ref, o_ref,
                 *, layer, step, chunk, sub, t_valid):
    j = pl.program_id(1)

    @pl.when(j == 0)
    def _():
        for hh in range(HG_HEADS):
            st_ref[hh] = s0_ref[0, hh].T

    lbp = lbp_ref[...]
    e = jnp.exp(lbp - jnp.max(lbp, axis=0, keepdims=True))
    lb_all = jnp.sum(e[:layer + 1], axis=0, keepdims=True) / jnp.sum(e, axis=0, keepdims=True)

    tri = tri_ref[...]
    row_c = lax.broadcasted_iota(I32, (chunk, HG_KDIM), 0)
    row_s = lax.broadcasted_iota(I32, (sub, HG_KDIM), 0)
    causal = (lax.broadcasted_iota(I32, (chunk, chunk), 0) >= lax.broadcasted_iota(I32, (chunk, chunk), 1))
    w = HG_WIDTH

    def chunk_body(ci, carry):
        r0 = pl.multiple_of(ci * chunk, chunk)
        rows = pl.ds(r0, chunk)
        valid = (j * step + r0 + row_c) < t_valid
        heads = []
        worst = jnp.float32(0.0)
        for hh in range(HG_HEADS):
            lanes = slice(hh * HG_KDIM, (hh + 1) * HG_KDIM)
            hq = ph_ref[0, rows, hh * HG_KDIM:(hh + 1) * HG_KDIM]
            z = ph_ref[0, rows, w + hh * HG_KDIM:w + (hh + 1) * HG_KDIM]
            v = ph_ref[0, rows, 2 * w + hh * HG_VDIM:2 * w + (hh + 1) * HG_VDIM]
            hg = ph_ref[0, rows, 3 * w + hh * HG_VDIM:3 * w + (hh + 1) * HG_VDIM]
            lb = lb_all[:, lanes]
            q = hq * _sigmoid(hq) * (HG_KDIM ** -0.5)
            logf = jnp.where(valid, jnp.log(lb + (1.0 - lb) * _sigmoid(z)), 0.0)
            kk = jnp.where(valid, (1.0 - lb) * _sigmoid(-z), 0.0)
            l1 = logf.astype(BF16)
            r1 = logf - l1.astype(F32)
            l2 = r1.astype(BF16)
            l3 = (r1 - l2.astype(F32)).astype(BF16)
            g = _dot(tri, l1) + _dot(tri, l2) + _dot(tri, l3)
            heads.append((q, kk, v, hg, g))
            worst = jnp.maximum(worst, jnp.max(-g[chunk - 1:chunk, :]))

        def finish(hh, o, kk, v, hg, g):
            g_last = g[chunk - 1:chunk, :]
            k_end = (kk * jnp.exp(g_last - g)).astype(BF16)
            st_ref[hh] = jnp.exp(g_last) * st_ref[hh] + _dot_tn(v.astype(BF16), k_end)
            gate = hg * _sigmoid(hg)
            rec_ref[0, rows, hh * HG_VDIM:(hh + 1) * HG_VDIM] = (_rms(o, gn_ref[...]) * gate).astype(BF16)

        @pl.when(worst <= HG_SAFE_DECAY)
        def _():
            for hh, (q, kk, v, hg, g) in enumerate(heads):
                qg = (q * jnp.exp(g)).astype(BF16)
                a = jnp.where(causal, _dot_nt(qg, (kk * jnp.exp(-g)).astype(BF16)), 0.0)
                o = _dot_nt(qg, st_ref[hh].astype(BF16)) + _dot(a.astype(BF16), v.astype(BF16))
                finish(hh, o, kk, v, hg, g)

        @pl.when(worst > HG_SAFE_DECAY)
        def _():
            for hh, (q, kk, v, hg, g) in enumerate(heads):
                vb = v.astype(BF16)
                o_ref[hh] = _dot_nt((q * jnp.exp(g)).astype(BF16), st_ref[hh].astype(BF16))
                for i in range(chunk // sub):
                    b0 = i * sub
                    gi, qi, ki_, vi = g[b0:b0 + sub], q[b0:b0 + sub], kk[b0:b0 + sub], v[b0:b0 + sub]
                    if i > 0:
                        ref = g[b0 - 1:b0, :]
                        q_rel = (qi * jnp.exp(gi - ref)).astype(BF16)
                        k_rel = (kk[:b0] * jnp.exp(ref - g[:b0])).astype(BF16)
                        a = _dot_nt(q_rel, k_rel)
                        o_ref[hh, b0:b0 + sub, :] += _dot(a.astype(BF16), vb[:b0])
                    for t in range(sub):
                        dec = jnp.exp(jnp.where(row_s <= t, gi[t:t + 1, :] - gi, -jnp.inf))
                        a_t = jnp.sum(qi[t:t + 1, :] * dec * ki_, axis=1, keepdims=True)
                        o_ref[hh, b0 + t:b0 + t + 1, :] += jnp.sum(a_t * vi, axis=0, keepdims=True)
                finish(hh, o_ref[hh], kk, v, hg, g)
        return carry

    lax.fori_loop(0, step // chunk, chunk_body, 0)

    @pl.when(j == pl.num_programs(1) - 1)
    def _():
        for hh in range(HG_HEADS):
            s_ref[0, hh] = st_ref[hh].T


def _hgrn(ph, lb_param, s0, hg_norm, layer, t_valid):
    b, t, _ = ph.shape
    step = min(HG_STEP, t)
    chunk = min(HG_CHUNK, step)
    sub = min(HG_SUB, chunk)
    assert t % step == 0 and step % chunk == 0 and chunk % sub == 0
    tri = jnp.asarray(np.tril(np.ones((chunk, chunk), np.float32)), BF16)
    return pl.pallas_call(
        functools.partial(_hgrn_kernel, layer=layer, step=step, chunk=chunk, sub=sub, t_valid=t_valid),
        grid=(b, t // step),
        in_specs=[pl.BlockSpec((1, step, 4 * HG_WIDTH), lambda i, j: (i, j, 0)),
                  _resident(lb_param.shape, 2),
                  pl.BlockSpec((1, HG_HEADS, HG_KDIM, HG_VDIM), lambda i, j: (i, 0, 0, 0)),
                  _resident((1, HG_VDIM), 2), _resident(tri.shape, 2)],
        out_specs=[pl.BlockSpec((1, step, HG_WIDTH), lambda i, j: (i, j, 0)),
                   pl.BlockSpec((1, HG_HEADS, HG_KDIM, HG_VDIM), lambda i, j: (i, 0, 0, 0))],
        out_shape=[jax.ShapeDtypeStruct((b, t, HG_WIDTH), BF16),
                   jax.ShapeDtypeStruct((b, HG_HEADS, HG_KDIM, HG_VDIM), F32)],
        scratch_shapes=[pltpu.VMEM((HG_HEADS, HG_VDIM, HG_KDIM), F32),
                        pltpu.VMEM((HG_HEADS, chunk, HG_VDIM), F32)],
        compiler_params=_params(2),
        name="hgrn",
    )(ph, lb_param, s0, hg_norm.reshape(1, HG_VDIM), tri)


def _sortable_key(score):
    bits = pltpu.bitcast(score, I32)
    return jnp.where(bits < 0, INT_MIN - bits, bits)


def _bit_transpose32(words):
    words = list(words)
    j, mask = 16, 0x0000FFFF
    while j:
        for k in range(32):
            if k & j == 0:
                t = (words[k] ^ lax.shift_right_logical(words[k + j], jnp.int32(j))) & mask
                words[k] = words[k] ^ t
                words[k + j] = words[k + j] ^ lax.shift_left(t, jnp.int32(j))
        j >>= 1
        mask = (mask ^ (mask << j)) & 0xFFFFFFFF
    return words


def _kth_largest(count_ge, topk, shape):
    zero = jnp.zeros(shape, I32)
    c = count_ge(zero)
    ok = c >= topk
    tau = jnp.where(ok, zero, jnp.full(shape, INT_MIN, I32))
    n_ge = jnp.where(ok, c, 0)

    def step(it, carry):
        tau, n_ge = carry
        cand = tau | lax.shift_left(jnp.int32(1), 30 - it)
        c = count_ge(cand)
        ok = c >= topk
        return jnp.where(ok, cand, tau), jnp.where(ok, c, n_ge)

    return lax.fori_loop(0, 31, step, (tau, n_ge))


def _dsa_prompt_kernel(qx_ref, qc_ref, wi_ref, kct_ref, kt_ref, v_ref, tri_ref, sel_ref, o_ref,
                       keys_ref, planes_ref, alive_ref, aq_ref, ac_ref, wb_ref, m_ref, l_ref, acc_ref, lg_ref, p_ref, bias_ref,
                       *, tq, ck, topk):
    i = pl.program_id(1)
    row0 = i * tq
    n_chunks = (row0 + tq + ck - 1) // ck
    n_sub = ck // LANES
    rows_q = N_HEADS * tq

    for h in range(IDX_HEADS):
        ac_ref[h * tq:(h + 1) * tq, :] = qc_ref[0, :, h * MXU_DIM:(h + 1) * MXU_DIM]
        aq_ref[h * tq:(h + 1) * tq, :] = qx_ref[0, :, h * MXU_DIM:(h + 1) * MXU_DIM]
        wb_ref[h] = jnp.broadcast_to(wi_ref[0, :, h:h + 1], (tq, LANES))

    row_id = row0 + lax.broadcasted_iota(I32, (tq, ck), 0)
    col_id = lax.broadcasted_iota(I32, (tq, ck), 1)

    def score_body(c, carry):
        col0 = pl.multiple_of(c * ck, ck)
        s = _dot(ac_ref[...], kct_ref[0, :, pl.ds(col0, ck)])
        tot = jnp.zeros((tq, ck), F32)
        for h in range(IDX_HEADS):
            tot = tot + jnp.maximum(s[h * tq:(h + 1) * tq, :], 0.0) * jnp.tile(wb_ref[h], (1, n_sub))
        key = jnp.where(col0 + col_id <= row_id, _sortable_key(tot), INT_MIN)
        keys_ref[:, pl.ds(col0, ck)] = key
        u = key.T ^ INT_MIN
        for blk in range(ck // PLANE_KEYS):
            slabs = [u[blk * PLANE_KEYS + 8 * k:blk * PLANE_KEYS + 8 * k + 8, :] for k in range(32)]
            slabs = _bit_transpose32(slabs)
            g0 = pl.multiple_of(c * (ck // 32) + blk * 8, 8)
            for o in range(32):
                planes_ref[o, pl.ds(g0, 8), :] = slabs[o]
        return carry

    @pl.when((pl.program_id(0) == 0) & (i == 0))
    def _():
        planes_ref[...] = jnp.zeros(planes_ref.shape, I32)

    lax.fori_loop(0, n_chunks, score_body, 0)

    n_groups = planes_ref.shape[1]
    group_id = lax.broadcasted_iota(I32, (n_groups, tq), 0)
    alive_ref[...] = jnp.where(group_id < n_chunks * (ck // 32), -1, 0)

    def ones_per_row(words):
        cnt = lax.population_count(words).reshape(n_groups // 8, 8, tq)
        return jnp.sum(jnp.sum(cnt, axis=0), axis=0, keepdims=True)

    def bit_step(it, carry):
        tau_u, n_above = carry
        alive = alive_ref[...]
        ones = alive & planes_ref[it]
        c1 = ones_per_row(ones)
        accept = (n_above + c1) >= topk
        alive_ref[...] = jnp.where(accept, ones, alive ^ ones)
        bit = lax.shift_left(jnp.int32(1), 31 - it)
        return jnp.where(accept, tau_u | bit, tau_u), jnp.where(accept, n_above, n_above + c1)

    zero = jnp.zeros((1, tq), I32)
    tau_u, n_gt = lax.fori_loop(0, 32, bit_step, (zero, zero))
    tau = tau_u ^ INT_MIN
    n_ge = jnp.where(tau > INT_MIN, n_gt + ones_per_row(alive_ref[...]), 0)

    def per_row(x):
        return jnp.broadcast_to(x, (LANES, tq)).T

    tau_lanes = jnp.maximum(tau, INT_MIN + 1)
    tau_sel = per_row(tau_lanes)
    has_ties = jnp.max(n_ge) > topk

    m_ref[...] = jnp.full(m_ref.shape, MASKED_MAX, F32)
    l_ref[...] = jnp.zeros(l_ref.shape, F32)
    acc_ref[...] = jnp.zeros(acc_ref.shape, F32)

    def attend(ranked):
        tau_t = jnp.tile(tau_sel, (1, n_sub))
        if ranked:
            need_eq = per_row((topk - n_gt).astype(F32))[:, 0:1]

        def body(c, eq_seen):
            col0 = pl.multiple_of(c * ck, ck)
            blk = keys_ref[:, pl.ds(col0, ck)]
            if ranked:
                eq = blk == tau_t
                eq_f = jnp.where(eq, 1.0, 0.0)
                before = _dot(eq_f.astype(BF16), tri_ref[...]) + eq_seen
                rank = jnp.where(eq, before, jnp.where(blk > tau_t, -1.0, jnp.inf))
                bias_ref[...] = jnp.where(rank < need_eq, 0.0, -jnp.inf)
                eq_seen = eq_seen + jnp.sum(eq_f, axis=1, keepdims=True)
            else:
                bias_ref[...] = jnp.where(blk >= tau_t, 0.0, -jnp.inf)
            for g in range(rows_q // ATT_ROWS):
                rows = slice(g * ATT_ROWS, (g + 1) * ATT_ROWS)
                off = (g * ATT_ROWS) % tq
                lg_ref[rows, :] = (_dot(aq_ref[rows, :], kt_ref[0, :, pl.ds(col0, ck)])
                                   + bias_ref[off:off + ATT_ROWS, :])
                m_old = m_ref[rows, :]
                m_new = jnp.maximum(m_old, jnp.max(lg_ref[rows, :], axis=1, keepdims=True))
                m_ref[rows, :] = m_new
                alpha = jnp.exp2(m_old - m_new)
                p = jnp.exp2(lg_ref[rows, :] - jnp.tile(m_new, (1, n_sub)))
                p_ref[rows, :] = p.astype(BF16)
                l_ref[rows, :] = alpha * l_ref[rows, :] + jnp.sum(p, axis=1, keepdims=True)
                acc_ref[rows, :] = (jnp.tile(alpha, (1, KV_WIDTH // LANES)) * acc_ref[rows, :]
                                    + _dot(p_ref[rows, :], v_ref[0, pl.ds(col0, ck), :]))
            return eq_seen
        lax.fori_loop(0, n_chunks, body, jnp.zeros((tq, 1), F32))

    @pl.when(has_ties)
    def _():
        attend(True)

    @pl.when(jnp.logical_not(has_ties))
    def _():
        attend(False)

    out = jnp.zeros((tq, ATT_WIDTH), F32)
    for h in range(N_HEADS):
        rows = slice(h * tq, (h + 1) * tq)
        o_h = acc_ref[rows, :] / jnp.tile(l_ref[rows, :], (1, KV_WIDTH // LANES))
        out = out + _dot(o_h.astype(BF16), sel_ref[h])
    o_ref[0] = out.astype(BF16)


def _dsa_prompt(qx, qc, wi, kct, kt, vb, sel):
    b, t, _ = qx.shape
    tq, ck = min(Q_TILE, t), min(K_CHUNK, t)
    assert t % tq == 0 and t % ck == 0 and ck % tq == 0
    topk = min(TOPK_MAX, t // 4)
    tri = jnp.asarray(np.triu(np.ones((ck, ck), np.float32), 1), BF16)
    tile = lambda width: pl.BlockSpec((1, tq, width), lambda i, j: (i, j, 0))
    whole = lambda rows, cols: pl.BlockSpec((1, rows, cols), lambda i, j: (i, 0, 0),
                                            pipeline_mode=pl.Buffered(1))
    return pl.pallas_call(
        functools.partial(_dsa_prompt_kernel, tq=tq, ck=ck, topk=topk),
        grid=(b, t // tq),
        in_specs=[tile(N_HEADS * MXU_DIM), tile(IDX_HEADS * MXU_DIM), tile(LANES),
                  whole(MXU_DIM, t), whole(MXU_DIM, t), whole(t, KV_WIDTH),
                  _resident(tri.shape, 2), _resident(sel.shape, 2)],
        out_specs=tile(ATT_WIDTH),
        out_shape=jax.ShapeDtypeStruct((b, t, ATT_WIDTH), BF16),
        scratch_shapes=[pltpu.VMEM((tq, t), I32),
                        pltpu.VMEM((32, t // 32, tq), I32),
                        pltpu.VMEM((t // 32, tq), I32),
                        pltpu.VMEM((N_HEADS * tq, MXU_DIM), BF16),
                        pltpu.VMEM((IDX_HEADS * tq, MXU_DIM), BF16),
                        pltpu.VMEM((IDX_HEADS, tq, LANES), F32),
                        pltpu.VMEM((N_HEADS * tq, LANES), F32),
                        pltpu.VMEM((N_HEADS * tq, LANES), F32),
                        pltpu.VMEM((N_HEADS * tq, KV_WIDTH), F32),
                        pltpu.VMEM((N_HEADS * tq, ck), F32),
                        pltpu.VMEM((N_HEADS * tq, ck), BF16),
                        pltpu.VMEM((tq, ck), F32)],
        compiler_params=_params(2),
        name="dsa_prompt",
    )(qx, qc, wi, kct, kt, vb, tri, sel)


def _dsa_sample_kernel(pt_ref, qx_ref, qc_ref, wb_ref, kcn_ref, kn_ref, vn_ref, tri_ref,
                       cik_hbm, ck_hbm, cv_hbm, o_ref,
                       ik_buf, k_buf, v_buf, keys_ref, bias_ref, lg_ref, sem,
                       *, layer, n_pages, page, chunk, tn, topk):
    b = pl.program_id(0)
    nb = pl.num_programs(0)
    past = n_pages * page
    width = past + LANES
    rows = tn * N_HEADS

    def copies(sample, slot, p):
        pg = pt_ref[sample, p]
        dst = pl.ds(pl.multiple_of(p * page, page), page)
        return (pltpu.make_async_copy(cik_hbm.at[layer, pg], ik_buf.at[slot, :, dst], sem.at[slot, 0]),
                pltpu.make_async_copy(ck_hbm.at[layer, pg], k_buf.at[slot, :, dst], sem.at[slot, 1]),
                pltpu.make_async_copy(cv_hbm.at[layer, pg], v_buf.at[slot, :, dst], sem.at[slot, 2]))

    def fetch(sample, slot):
        def body(p, carry):
            for cp in copies(sample, slot, p):
                cp.start()
            return carry
        lax.fori_loop(0, n_pages, body, 0)

    def wait(sample, slot):
        def body(p, carry):
            for cp in copies(sample, slot, p):
                cp.wait()
            return carry
        lax.fori_loop(0, n_pages, body, 0)

    slot = b % 2

    @pl.when(b == 0)
    def _():
        fetch(0, 0)

    @pl.when(b + 1 < nb)
    def _():
        fetch(b + 1, 1 - slot)

    wait(b, slot)

    qc = qc_ref[0]
    qx = qx_ref[0]
    wb = wb_ref[0]
    tok = lax.broadcasted_iota(I32, (tn, LANES), 0)
    col = lax.broadcasted_iota(I32, (tn, LANES), 1)

    def head_sum(s):
        n = s.shape[1]
        r = jnp.maximum(s, 0.0) * jnp.tile(wb, (1, n // LANES))
        return jnp.sum(r.reshape(tn, N_HEADS, n), axis=1)

    keys_ref[...] = jnp.full(keys_ref.shape, INT_MIN, I32)

    zero_rows = jnp.zeros((MXU_DIM - 3 * IDX_DIM, chunk), BF16)
    for c in range(past // chunk):
        cols = slice(c * chunk, (c + 1) * chunk)
        hi, lo = _split2(ik_buf[slot, :, cols])
        kct = jnp.concatenate([hi, hi, lo, zero_rows], axis=0)
        keys_ref[0:tn, cols] = _sortable_key(head_sum(_dot(qc, kct)))
    s_new = head_sum(_dot(qc, kcn_ref[0]))
    keys_ref[0:tn, past:width] = jnp.where(col <= tok, _sortable_key(s_new), INT_MIN)

    def count_ge(cand):
        hit = jnp.where(keys_ref[...] >= cand, 1, 0)
        parts = [hit[:, u * LANES:(u + 1) * LANES] for u in range(width // LANES)]
        while len(parts) > 1:
            parts = [a + b for a, b in zip(parts[0::2], parts[1::2])] + parts[len(parts) & ~1:]
        return jnp.sum(parts[0], axis=1, keepdims=True)

    tau, n_ge = _kth_largest(count_ge, topk, (8, 1))
    tau_sel = jnp.maximum(tau, INT_MIN + 1)
    has_ties = jnp.max(n_ge) > topk

    @pl.when(jnp.logical_not(has_ties))
    def _():
        bias_ref[...] = jnp.where(keys_ref[...] >= tau_sel, 0.0, -jnp.inf)

    @pl.when(has_ties)
    def _():
        n_gt = jnp.sum(jnp.where(keys_ref[...] > tau_sel, 1, 0), axis=1, keepdims=True)
        need_eq = (topk - n_gt).astype(F32)

        def body(u, eq_seen):
            c0 = pl.multiple_of(u * LANES, LANES)
            blk = keys_ref[:, pl.ds(c0, LANES)]
            eq = blk == tau_sel
            eq_f = jnp.where(eq, 1.0, 0.0)
            before = _dot(eq_f.astype(BF16), tri_ref[...]) + eq_seen
            rank = jnp.where(eq, before, jnp.where(blk > tau_sel, -1.0, jnp.inf))
            bias_ref[:, pl.ds(c0, LANES)] = jnp.where(rank < need_eq, 0.0, -jnp.inf)
            return eq_seen + jnp.sum(eq_f, axis=1, keepdims=True)

        lax.fori_loop(0, width // LANES, body, jnp.zeros((8, 1), F32))

    def expand(x):
        n = x.shape[1]
        return jnp.broadcast_to(x[0:tn].reshape(tn, 1, n), (tn, N_HEADS, n)).reshape(rows, n)

    for c in range(past // chunk):
        cols = slice(c * chunk, (c + 1) * chunk)
        lg_ref[:, cols] = _dot(qx, k_buf[slot, :, cols].astype(BF16)) + expand(bias_ref[:, cols])
    lg_ref[:, past:width] = _dot(qx, kn_ref[0]) + expand(bias_ref[:, past:width])
    lg = lg_ref[...]
    p = jnp.exp2(lg - jnp.max(lg, axis=1, keepdims=True))
    l = jnp.sum(p, axis=1, keepdims=True)
    lg_ref[...] = p
    acc = _dot_nt(lg_ref[:, past:width].astype(BF16), vn_ref[0])
    for c in range(past // chunk):
        cols = slice(c * chunk, (c + 1) * chunk)
        acc = acc + _dot_nt(lg_ref[:, cols].astype(BF16), v_buf[slot, :, cols].astype(BF16))
    o_ref[0] = acc / l


def _dsa_sample(page_table, qx, qc, wb, kcn, kn, vn, cache_ik, cache_k, cache_v, layer):
    db, rows, _ = qx.shape
    tn = rows // N_HEADS
    n_pages = page_table.shape[1]
    page = cache_ik.shape[3]
    past = n_pages * page
    chunk = min(SAMPLE_CHUNK, past)
    assert page == LANES and tn <= 8 and past % chunk == 0
    tri = jnp.asarray(np.triu(np.ones((page, page), np.float32), 1), BF16)
    topk = min(TOPK_MAX, (past + tn) // 4)
    per = lambda r, c: pl.BlockSpec((1, r, c), lambda i, pt: (i, 0, 0))
    grid_spec = pltpu.PrefetchScalarGridSpec(
        num_scalar_prefetch=1,
        grid=(db,),
        in_specs=[per(rows, MXU_DIM), per(rows, MXU_DIM), per(rows, LANES),
                  per(MXU_DIM, LANES), per(KV_WIDTH, LANES), per(KV_WIDTH, LANES),
                  pl.BlockSpec(tri.shape, lambda i, pt: (0, 0)),
                  pl.BlockSpec(memory_space=pl.ANY), pl.BlockSpec(memory_space=pl.ANY),
                  pl.BlockSpec(memory_space=pl.ANY)],
        out_specs=per(rows, KV_WIDTH),
        scratch_shapes=[pltpu.VMEM((2, IDX_DIM, past), F32),
                        pltpu.VMEM((2, KV_WIDTH, past), F32),
                        pltpu.VMEM((2, KV_WIDTH, past), F32),
                        pltpu.VMEM((8, past + LANES), I32),
                        pltpu.VMEM((8, past + LANES), F32),
                        pltpu.VMEM((rows, past + LANES), F32),
                        pltpu.SemaphoreType.DMA((2, 3))],
    )
    return pl.pallas_call(
        functools.partial(_dsa_sample_kernel, layer=layer, n_pages=n_pages, page=page, chunk=chunk,
                          tn=tn, topk=topk),
        grid_spec=grid_spec,
        out_shape=jax.ShapeDtypeStruct((db, rows, KV_WIDTH), F32),
        compiler_params=_params(1),
        name="dsa_sample",
    )(page_table, qx, qc, wb, kcn, kn, vn, tri, cache_ik, cache_k, cache_v)


def _merge_ffn_kernel(x_ref, att_ref, rec_ref, pg_ref, wa_ref, wh_ref, wo_ref,
                      g_ref, wg_ref, wu_ref, wd_ref, o_ref):
    d = x_ref.shape[1]
    pg = pg_ref[...]
    merged = (_sigmoid(pg[:, :d]) * _dot(att_ref[...], wa_ref[...])
              + _sigmoid(pg[:, d:]) * _dot(rec_ref[...], wh_ref[...]))
    x = x_ref[...] + _dot(merged.astype(BF16), wo_ref[...])
    o_ref[...] = _ffn_apply(x, g_ref, wg_ref, wu_ref, wd_ref)


def _merge_ffn(x, att, rec, pg, wa, wh, wo, norm, wg, wu, wd):
    n, d = x.shape
    tm = min(ROW_TILE, n)
    assert n % tm == 0 and wg.shape[1] % FFN_CHUNK == 0
    row = lambda width: pl.BlockSpec((tm, width), lambda i: (i, 0))
    weights = [wa, wh, wo, norm.reshape(1, d), wg, wu, wd]
    return pl.pallas_call(
        _merge_ffn_kernel,
        grid=(n // tm,),
        in_specs=[row(d), row(ATT_WIDTH), row(HG_WIDTH), row(2 * d)] + [_resident(a.shape, 1) for a in weights],
        out_specs=row(d),
        out_shape=jax.ShapeDtypeStruct((n, d), F32),
        compiler_params=_params(1),
        name="merge_ffn",
    )(x, att, rec, pg, *weights)


def _prep_layer(l, ffn1_norm, ffn1_w_gate, ffn1_w_up, ffn1_w_down, mix_norm, w_in, q_norm, k_norm, hg_norm,
                w_branch_attn, w_branch_hgrn, w_out, ffn2_norm, ffn2_w_gate, ffn2_w_up, ffn2_w_down):
    d = w_in.shape[1]
    w = w_in[l]
    off = [0]

    def take(width):
        a = w[:, off[0]:off[0] + width]
        off[0] += width
        return a

    wq, wk, wv = take(ATT_WIDTH), take(KV_WIDTH), take(KV_WIDTH)
    wqi, wki, wwi = take(IDX_HEADS * IDX_DIM), take(IDX_DIM), take(IDX_HEADS)
    wh = take(4 * HG_WIDTH)
    wg = take(2 * d)
    zeros = jnp.zeros((d, IDX_DIM), F32)
    wqi_heads = wqi.reshape(d, IDX_HEADS, IDX_DIM)
    wqi_arr = jnp.concatenate([wqi_heads, wqi_heads, wqi_heads, jnp.zeros_like(wqi_heads)], axis=2)
    wki_arr = jnp.concatenate([wki, wki, wki, zeros], axis=1)
    wwi_arr = jnp.concatenate([wwi, jnp.zeros((d, LANES - IDX_HEADS), F32)], axis=1)
    bd = np.kron(np.eye(ATT_WIDTH // HEAD_DIM, dtype=np.float32), np.ones((HEAD_DIM, HEAD_DIM), np.float32))
    pq = np.zeros((ATT_WIDTH, N_HEADS * MXU_DIM), np.float32)
    sel = np.zeros((N_HEADS, KV_WIDTH, ATT_WIDTH), np.float32)
    for h in range(N_HEADS):
        g = h // (N_HEADS // KV_HEADS)
        for e in range(HEAD_DIM):
            pq[h * HEAD_DIM + e, h * MXU_DIM + g * HEAD_DIM + e] = 1.0
            sel[h, g * HEAD_DIM + e, h * HEAD_DIM + e] = 1.0
    bf = lambda a: a.astype(BF16)
    return dict(
        ffn1=(ffn1_norm[l], bf(ffn1_w_gate[l]), bf(ffn1_w_up[l]), bf(ffn1_w_down[l])),
        ffn2=(ffn2_norm[l], bf(ffn2_w_gate[l]), bf(ffn2_w_up[l]), bf(ffn2_w_down[l])),
        proj=dict(mix_norm=mix_norm[l].reshape(1, d), wq=bf(wq), wk=bf(wk), wv=bf(wv),
                  wqi=bf(wqi_arr.reshape(d, IDX_HEADS * MXU_DIM)), wki=bf(wki_arr), wwi=bf(wwi_arr),
                  wh=bf(wh), wg=bf(wg), bd=jnp.asarray(bd, BF16), pq=jnp.asarray(pq, BF16),
                  qn=jnp.tile(q_norm[l], N_HEADS).reshape(1, ATT_WIDTH),
                  kn=jnp.tile(k_norm[l], KV_HEADS).reshape(1, KV_WIDTH)),
        sel=jnp.asarray(sel, BF16),
        hg_norm=hg_norm[l],
        merge=(bf(w_branch_attn[l]), bf(w_branch_hgrn[l]), bf(w_out[l])),
    )


def kernel(x_prompt, x_sample, cache_k, cache_v, cache_idx_k, state_hgrn, page_table, hg_lower_bound,
           ffn1_norm, ffn1_w_gate, ffn1_w_up, ffn1_w_down, mix_norm, w_in, q_norm, k_norm, hg_norm,
           w_branch_attn, w_branch_hgrn, w_out, ffn2_norm, ffn2_w_gate, ffn2_w_up, ffn2_w_down):
    b, t, d = x_prompt.shape
    db, tn, _ = x_sample.shape
    depth = w_in.shape[0]
    n_pool, page = cache_k.shape[1], cache_k.shape[2]
    past = page_table.shape[1] * page
    cache_ikt = cache_idx_k.transpose(0, 1, 3, 2)
    cache_kt = cache_k.transpose(0, 1, 3, 4, 2).reshape(depth, n_pool, KV_WIDTH, page)
    cache_vt = cache_v.transpose(0, 1, 3, 4, 2).reshape(depth, n_pool, KV_WIDTH, page)
    lb_param = hg_lower_bound.astype(F32)
    tn_pad = -(-tn // HG_SUB) * HG_SUB

    xp = x_prompt.reshape(b * t, d)
    xs = x_sample.reshape(db * tn, d)
    outs = [[] for _ in range(8)]
    for l in range(depth):
        w = _prep_layer(l, ffn1_norm, ffn1_w_gate, ffn1_w_up, ffn1_w_down, mix_norm, w_in, q_norm, k_norm,
                        hg_norm, w_branch_attn, w_branch_hgrn, w_out, ffn2_norm, ffn2_w_gate, ffn2_w_up,
                        ffn2_w_down)
        xp = _ffn(xp, *w["ffn1"])
        qx, k, v, vb, qc, ki, wi, ph, pg, kt, kct = _proj(xp, b, t, 0, w["proj"], transposed=True)
        att = _dsa_prompt(qx.reshape(b, t, -1), qc.reshape(b, t, -1), wi.reshape(b, t, -1),
                          kct, kt, vb.reshape(b, t, -1), w["sel"])
        rec, s_p = _hgrn(ph.reshape(b, t, -1), lb_param, jnp.zeros((b, HG_HEADS, HG_KDIM, HG_VDIM), F32),
                         w["hg_norm"], l, t)
        xp = _merge_ffn(xp, att.reshape(b * t, -1), rec.reshape(b * t, -1), pg, *w["merge"], *w["ffn2"])
        outs[0].append(k.reshape(b, t, KV_HEADS, HEAD_DIM))
        outs[1].append(v.reshape(b, t, KV_HEADS, HEAD_DIM))
        outs[2].append(ki.reshape(b, t, IDX_DIM))
        outs[3].append(s_p.astype(x_prompt.dtype))

        xs = _ffn(xs, *w["ffn1"])
        qx, k, v, vb, qc, ki, wi, ph, pg, kb, kc = _proj(xs, db, tn, past, w["proj"], transposed=False)
        rows = tn * N_HEADS
        qx3 = qx.reshape(db, rows, MXU_DIM)
        qc3 = qc.reshape(db, rows, MXU_DIM)
        wb3 = jnp.broadcast_to(wi[:, :IDX_HEADS].reshape(db, rows, 1), (db, rows, LANES))
        pad_new = lambda a: jnp.pad(a.reshape(db, tn, -1), ((0, 0), (0, page - tn), (0, 0))).transpose(0, 2, 1)
        o_s = _dsa_sample(page_table, qx3, qc3, wb3, pad_new(kc), pad_new(kb), pad_new(vb),
                          cache_ikt, cache_kt, cache_vt, l)
        o_s = o_s.reshape(db, tn, KV_HEADS, N_HEADS // KV_HEADS, KV_HEADS, HEAD_DIM)
        att_s = jnp.stack([o_s[:, :, g, :, g, :] for g in range(KV_HEADS)], axis=2)
        att_s = att_s.reshape(db * tn, ATT_WIDTH).astype(BF16)
        ph3 = jnp.pad(ph.reshape(db, tn, -1), ((0, 0), (0, tn_pad - tn), (0, 0)))
        rec, s_s = _hgrn(ph3, lb_param, state_hgrn[l], w["hg_norm"], l, tn)
        rec = rec[:, :tn].reshape(db * tn, HG_WIDTH)
        xs = _merge_ffn(xs, att_s, rec, pg, *w["merge"], *w["ffn2"])
        outs[4].append(k.reshape(db, tn, KV_HEADS, HEAD_DIM))
        outs[5].append(v.reshape(db, tn, KV_HEADS, HEAD_DIM))
        outs[6].append(ki.reshape(db, tn, IDX_DIM))
        outs[7].append(s_s.astype(state_hgrn.dtype))

    return (xp.reshape(b, t, d), xs.reshape(db, tn, d)) + tuple(jnp.stack(o) for o in outs)
```

```python
import functools

import numpy as np
import jax
import jax.numpy as jnp
from jax import lax
from jax.experimental import pallas as pl
from jax.experimental.pallas import tpu as pltpu

F32, BF16, I32 = jnp.float32, jnp.bfloat16, jnp.int32

N_HEADS, KV_HEADS, HEAD_DIM = 8, 4, 64
IDX_HEADS, IDX_DIM, TOPK_MAX = 8, 64, 256
HG_HEADS, HG_KDIM, HG_VDIM = 4, 128, 128
ATT_WIDTH = N_HEADS * HEAD_DIM
KV_WIDTH = KV_HEADS * HEAD_DIM
HG_WIDTH = HG_HEADS * HG_KDIM
ROPE_THETA = 10000.0
NORM_EPS = 1e-6

LANES = 128
MXU_DIM = 256
VMEM_LIMIT_BYTES = 56 * 2**20
INT_MIN = -2**31
MASKED_MAX = -1e30
LOG2E = 1.4426950408889634

ROW_TILE = 512
PROJ_ROW_TILE = 256
FFN_CHUNK = 256
HG_STEP = 256
HG_CHUNK = 64
HG_GROUP = 2
HG_SAFE_DECAY = 80.0
HG_SUB = 16
Q_TILE = 256
K_CHUNK = 512
ATT_ROWS = 256
PLANE_KEYS = 256
SAMPLE_CHUNK = 2048


def _params(n_axes):
    return pltpu.CompilerParams(dimension_semantics=("arbitrary",) * n_axes,
                                vmem_limit_bytes=VMEM_LIMIT_BYTES)


def _resident(shape, n_axes):
    zeros = (0,) * len(shape)
    if n_axes == 1:
        return pl.BlockSpec(shape, lambda i: zeros, pipeline_mode=pl.Buffered(1))
    return pl.BlockSpec(shape, lambda i, j: zeros, pipeline_mode=pl.Buffered(1))


def _rms(x, gain):
    return x * lax.rsqrt(jnp.mean(x * x, axis=-1, keepdims=True) + NORM_EPS) * gain


def _sigmoid(x):
    return 1.0 / (1.0 + jnp.exp(-x))


def _dot(a, b):
    return jnp.dot(a, b, preferred_element_type=F32)


def _dot_nt(a, b):
    return lax.dot_general(a, b, (((1,), (1,)), ((), ())), preferred_element_type=F32)


def _dot_tn(a, b):
    return lax.dot_general(a, b, (((0,), (0,)), ((), ())), preferred_element_type=F32)


def _split2(x):
    hi = x.astype(BF16)
    return hi, (x - hi.astype(F32)).astype(BF16)


def _ffn_apply(x, g_ref, wg_ref, wu_ref, wd_ref):
    h = _rms(x, g_ref[...]).astype(BF16)
    d_ff = wg_ref.shape[1]
    acc = jnp.zeros(x.shape, F32)
    for c in range(d_ff // FFN_CHUNK):
        cols = slice(c * FFN_CHUNK, (c + 1) * FFN_CHUNK)
        g = _dot(h, wg_ref[:, cols])
        u = _dot(h, wu_ref[:, cols])
        a = (g * _sigmoid(g) * u).astype(BF16)
        acc = acc + _dot(a, wd_ref[cols, :])
    return x + 0.5 * acc


def _ffn_kernel(x_ref, g_ref, wg_ref, wu_ref, wd_ref, o_ref):
    o_ref[...] = _ffn_apply(x_ref[...], g_ref, wg_ref, wu_ref, wd_ref)


def _ffn(x, norm, wg, wu, wd):
    n, d = x.shape
    tm = min(ROW_TILE, n)
    assert n % tm == 0 and wg.shape[1] % FFN_CHUNK == 0
    return pl.pallas_call(
        _ffn_kernel,
        grid=(n // tm,),
        in_specs=[pl.BlockSpec((tm, d), lambda i: (i, 0)), _resident((1, d), 1),
                  _resident(wg.shape, 1), _resident(wu.shape, 1), _resident(wd.shape, 1)],
        out_specs=pl.BlockSpec((tm, d), lambda i: (i, 0)),
        out_shape=jax.ShapeDtypeStruct((n, d), F32),
        compiler_params=_params(1),
        name="ffn",
    )(x, norm.reshape(1, d), wg, wu, wd)


def _rope(y, cos, sin):
    n = y.shape[1]
    reps = n // LANES
    c = jnp.tile(cos, (1, reps)) if reps > 1 else cos
    s = jnp.tile(sin, (1, reps)) if reps > 1 else sin
    lane = lax.broadcasted_iota(I32, y.shape, 1)
    first_half = (lane & (HEAD_DIM // 2)) == 0
    partner = jnp.where(first_half, pltpu.roll(y, n - HEAD_DIM // 2, axis=1),
                        pltpu.roll(y, HEAD_DIM // 2, axis=1))
    return y * c + partner * s


def _proj_kernel(x_ref, gn_ref, wq_ref, wk_ref, wv_ref, wqi_ref, wki_ref, wwi_ref, wh_ref, wg_ref,
                 bd_ref, pq_ref, qn_ref, kn_ref, cos_ref, sin_ref,
                 qx_ref, k_ref, v_ref, vb_ref, qc_ref, ki_ref, wi_ref, ph_ref, pg_ref, kb_ref, kc_ref,
                 *, transposed):
    h = _rms(x_ref[...], gn_ref[...]).astype(BF16)
    cos, sin = cos_ref[...], sin_ref[...]

    def head_norm(y, gain):
        n = y.shape[1]
        hi, lo = _split2(y * y)
        bd = bd_ref[:n, :n]
        ss = _dot(hi, bd) + _dot(lo, bd)
        return y * lax.rsqrt(ss * (1.0 / HEAD_DIM) + NORM_EPS) * gain

    q = _rope(head_norm(_dot(h, wq_ref[...]), qn_ref[...]), cos, sin) * (LOG2E * HEAD_DIM ** -0.5)
    qx_ref[...] = _dot(q.astype(BF16), pq_ref[...]).astype(BF16)

    k = _rope(head_norm(_dot(h, wk_ref[...]), kn_ref[...]), cos, sin)
    k_ref[...] = k
    v = _dot(h, wv_ref[...])
    v_ref[...] = v
    vb_ref[...] = v.astype(BF16)

    qi = _rope(_dot(h, wqi_ref[...]), cos, sin) * (IDX_DIM ** -0.5)
    lane = lax.broadcasted_iota(I32, qi.shape, 1) & (MXU_DIM - 1)
    is_lo = (lane >= IDX_DIM) & (lane < 2 * IDX_DIM)
    qc_ref[...] = jnp.where(is_lo, qi - qi.astype(BF16).astype(F32), qi).astype(BF16)

    ki = _rope(_dot(h, wki_ref[...]), cos, sin)
    ki_ref[...] = ki[:, :IDX_DIM]
    lane = lax.broadcasted_iota(I32, ki.shape, 1)
    is_lo = (lane >= 2 * IDX_DIM) & (lane < 3 * IDX_DIM)
    kc = jnp.where(is_lo, ki - ki.astype(BF16).astype(F32), ki)

    if transposed:
        kb_ref[0] = k.T.astype(BF16)
        kc_ref[0] = kc.T.astype(BF16)
    else:
        kb_ref[...] = k.astype(BF16)
        kc_ref[...] = kc.astype(BF16)

    wi_ref[...] = _dot(h, wwi_ref[...]) * (IDX_HEADS ** -0.5)
    ph_ref[...] = _dot(h, wh_ref[...])
    pg_ref[...] = _dot(h, wg_ref[...])


def _proj(x, batch, seq, pos0, w, transposed):
    n, d = x.shape
    tm = min(PROJ_ROW_TILE, n)
    assert n % tm == 0
    lane = np.arange(LANES)
    inv = ROPE_THETA ** (-(lane % (HEAD_DIM // 2)).astype(np.float32) * 2.0 / HEAD_DIM)
    sign = np.where((lane % HEAD_DIM) < HEAD_DIM // 2, -1.0, 1.0).astype(np.float32)
    ang = (pos0 + jnp.arange(seq)).astype(F32)[:, None] * jnp.asarray(inv)[None, :]
    cos_t, sin_t = jnp.cos(ang), jnp.sin(ang) * jnp.asarray(sign)[None, :]
    if seq >= tm:
        assert seq % tm == 0
        per_seq = seq // tm
        tab_map = lambda i: (i % per_seq, 0)
    else:
        assert tm % seq == 0
        cos_t, sin_t = jnp.tile(cos_t, (tm // seq, 1)), jnp.tile(sin_t, (tm // seq, 1))
        tab_map = lambda i: (0, 0)

    row = lambda width: pl.BlockSpec((tm, width), lambda i: (i, 0))
    if transposed:
        per_seq = seq // tm
        kt_spec = pl.BlockSpec((1, MXU_DIM, tm), lambda i: (i // per_seq, 0, i % per_seq))
        kt_shape = jax.ShapeDtypeStruct((batch, MXU_DIM, seq), BF16)
    else:
        kt_spec = row(MXU_DIM)
        kt_shape = jax.ShapeDtypeStruct((n, MXU_DIM), BF16)

    weights = [w["wq"], w["wk"], w["wv"], w["wqi"], w["wki"], w["wwi"], w["wh"], w["wg"],
               w["bd"], w["pq"], w["qn"], w["kn"]]
    out_shape = [
        jax.ShapeDtypeStruct((n, N_HEADS * MXU_DIM), BF16),
        jax.ShapeDtypeStruct((n, KV_WIDTH), F32),
        jax.ShapeDtypeStruct((n, KV_WIDTH), F32),
        jax.ShapeDtypeStruct((n, KV_WIDTH), BF16),
        jax.ShapeDtypeStruct((n, IDX_HEADS * MXU_DIM), BF16),
        jax.ShapeDtypeStruct((n, IDX_DIM), F32),
        jax.ShapeDtypeStruct((n, LANES), F32),
        jax.ShapeDtypeStruct((n, 4 * HG_WIDTH), F32),
        jax.ShapeDtypeStruct((n, 2 * d), F32),
        kt_shape,
        kt_shape,
    ]
    out_specs = [row(N_HEADS * MXU_DIM), row(KV_WIDTH), row(KV_WIDTH), row(KV_WIDTH),
                 row(IDX_HEADS * MXU_DIM), row(IDX_DIM), row(LANES), row(4 * HG_WIDTH), row(2 * d),
                 kt_spec, kt_spec]
    return pl.pallas_call(
        functools.partial(_proj_kernel, transposed=transposed),
        grid=(n // tm,),
        in_specs=[row(d), _resident((1, d), 1)] + [_resident(a.shape, 1) for a in weights]
                 + [pl.BlockSpec((tm, LANES), tab_map), pl.BlockSpec((tm, LANES), tab_map)],
        out_specs=out_specs,
        out_shape=out_shape,
        compiler_params=_params(1),
        name="proj",
    )(x, w["mix_norm"], *weights, cos_t, sin_t)


def _hgrn_kernel(ph_ref, lbp_ref, s0_ref, gn_ref, tri_ref, rec_ref, s_ref, st_ref, o_ref,
                 *, layer, step, chunk, sub, t_valid):
    j = pl.program_id(1)

    @pl.when(j == 0)
    def _():
        for hh in range(HG_HEADS):
            st_ref[hh] = s0_ref[0, hh].T

    lbp = lbp_ref[...]
    e = jnp.exp(lbp - jnp.max(lbp, axis=0, keepdims=True))
    lb_all = jnp.sum(e[:layer + 1], axis=0, keepdims=True) / jnp.sum(e, axis=0, keepdims=True)

    tri = tri_ref[...]
    row_c = lax.broadcasted_iota(I32, (chunk, HG_KDIM), 0)
    row_s = lax.broadcasted_iota(I32, (sub, HG_KDIM), 0)
    causal = (lax.broadcasted_iota(I32, (chunk, chunk), 0) >= lax.broadcasted_iota(I32, (chunk, chunk), 1))
    w = HG_WIDTH

    group = HG_GROUP if (step // chunk) % HG_GROUP == 0 else 1

    def chunk_body(ci, carry):
        chunks = []
        worst = jnp.float32(0.0)
        for cc in range(group):
            r0 = pl.multiple_of((ci * group + cc) * chunk, chunk)
            rows = pl.ds(r0, chunk)
            valid = (j * step + r0 + row_c) < t_valid
            heads = []
            for hh in range(HG_HEADS):
                lanes = slice(hh * HG_KDIM, (hh + 1) * HG_KDIM)
                hq = ph_ref[0, rows, hh * HG_KDIM:(hh + 1) * HG_KDIM]
                z = ph_ref[0, rows, w + hh * HG_KDIM:w + (hh + 1) * HG_KDIM]
                v = ph_ref[0, rows, 2 * w + hh * HG_VDIM:2 * w + (hh + 1) * HG_VDIM]
                hg = ph_ref[0, rows, 3 * w + hh * HG_VDIM:3 * w + (hh + 1) * HG_VDIM]
                lb = lb_all[:, lanes]
                q = hq * _sigmoid(hq) * (HG_KDIM ** -0.5)
                logf = jnp.where(valid, jnp.log(lb + (1.0 - lb) * _sigmoid(z)), 0.0)
                kk = jnp.where(valid, (1.0 - lb) * _sigmoid(-z), 0.0)
                l1 = logf.astype(BF16)
                r1 = logf - l1.astype(F32)
                l2 = r1.astype(BF16)
                l3 = (r1 - l2.astype(F32)).astype(BF16)
                g = _dot(tri, l1) + _dot(tri, l2) + _dot(tri, l3)
                heads.append((q, kk, v, hg, g))
                worst = jnp.maximum(worst, jnp.max(-g[chunk - 1:chunk, :]))
            chunks.append((rows, heads))

        def finish(rows, hh, o, kk, v, hg, g):
            g_last = g[chunk - 1:chunk, :]
            k_end = (kk * jnp.exp(g_last - g)).astype(BF16)
            st_ref[hh] = jnp.exp(g_last) * st_ref[hh] + _dot_tn(v.astype(BF16), k_end)
            gate = hg * _sigmoid(hg)
            rec_ref[0, rows, hh * HG_VDIM:(hh + 1) * HG_VDIM] = (_rms(o, gn_ref[...]) * gate).astype(BF16)

        @pl.when(worst <= HG_SAFE_DECAY)
        def _():
            for rows, heads in chunks:
                for hh, (q, kk, v, hg, g) in enumerate(heads):
                    qg = (q * jnp.exp(g)).astype(BF16)
                    a = jnp.where(causal, _dot_nt(qg, (kk * jnp.exp(-g)).astype(BF16)), 0.0)
                    o = _dot_nt(qg, st_ref[hh].astype(BF16)) + _dot(a.astype(BF16), v.astype(BF16))
                    finish(rows, hh, o, kk, v, hg, g)

        @pl.when(worst > HG_SAFE_DECAY)
        def _():
            for rows, heads in chunks:
                for hh, (q, kk, v, hg, g) in enumerate(heads):
                    vb = v.astype(BF16)
                    o_ref[hh] = _dot_nt((q * jnp.exp(g)).astype(BF16), st_ref[hh].astype(BF16))
                    for i in range(chunk // sub):
                        b0 = i * sub
                        gi, qi, ki_, vi = g[b0:b0 + sub], q[b0:b0 + sub], kk[b0:b0 + sub], v[b0:b0 + sub]
                        if i > 0:
                            ref = g[b0 - 1:b0, :]
                            q_rel = (qi * jnp.exp(gi - ref)).astype(BF16)
                            k_rel = (kk[:b0] * jnp.exp(ref - g[:b0])).astype(BF16)
                            a = _dot_nt(q_rel, k_rel)
                            o_ref[hh, b0:b0 + sub, :] += _dot(a.astype(BF16), vb[:b0])
                        for t in range(sub):
                            dec = jnp.exp(jnp.where(row_s <= t, gi[t:t + 1, :] - gi, -jnp.inf))
                            a_t = jnp.sum(qi[t:t + 1, :] * dec * ki_, axis=1, keepdims=True)
                            o_ref[hh, b0 + t:b0 + t + 1, :] += jnp.sum(a_t * vi, axis=0, keepdims=True)
                    finish(rows, hh, o_ref[hh], kk, v, hg, g)
        return carry

    lax.fori_loop(0, step // chunk // group, chunk_body, 0)

    @pl.when(j == pl.num_programs(1) - 1)
    def _():
        for hh in range(HG_HEADS):
            s_ref[0, hh] = st_ref[hh].T


def _hgrn(ph, lb_param, s0, hg_norm, layer, t_valid):
    b, t, _ = ph.shape
    step = min(HG_STEP, t)
    chunk = min(HG_CHUNK, step)
    sub = min(HG_SUB, chunk)
    assert t % step == 0 and step % chunk == 0 and chunk % sub == 0
    tri = jnp.asarray(np.tril(np.ones((chunk, chunk), np.float32)), BF16)
    return pl.pallas_call(
        functools.partial(_hgrn_kernel, layer=layer, step=step, chunk=chunk, sub=sub, t_valid=t_valid),
        grid=(b, t // step),
        in_specs=[pl.BlockSpec((1, step, 4 * HG_WIDTH), lambda i, j: (i, j, 0)),
                  _resident(lb_param.shape, 2),
                  pl.BlockSpec((1, HG_HEADS, HG_KDIM, HG_VDIM), lambda i, j: (i, 0, 0, 0)),
                  _resident((1, HG_VDIM), 2), _resident(tri.shape, 2)],
        out_specs=[pl.BlockSpec((1, step, HG_WIDTH), lambda i, j: (i, j, 0)),
                   pl.BlockSpec((1, HG_HEADS, HG_KDIM, HG_VDIM), lambda i, j: (i, 0, 0, 0))],
        out_shape=[jax.ShapeDtypeStruct((b, t, HG_WIDTH), BF16),
                   jax.ShapeDtypeStruct((b, HG_HEADS, HG_KDIM, HG_VDIM), F32)],
        scratch_shapes=[pltpu.VMEM((HG_HEADS, HG_VDIM, HG_KDIM), F32),
                        pltpu.VMEM((HG_HEADS, chunk, HG_VDIM), F32)],
        compiler_params=_params(2),
        name="hgrn",
    )(ph, lb_param, s0, hg_norm.reshape(1, HG_VDIM), tri)


def _sortable_key(score):
    bits = pltpu.bitcast(score, I32)
    return jnp.where(bits < 0, INT_MIN - bits, bits)


def _bit_transpose32(words):
    words = list(words)
    j, mask = 16, 0x0000FFFF
    while j:
        for k in range(32):
            if k & j == 0:
                t = (words[k] ^ lax.shift_right_logical(words[k + j], jnp.int32(j))) & mask
                words[k] = words[k] ^ t
                words[k + j] = words[k + j] ^ lax.shift_left(t, jnp.int32(j))
        j >>= 1
        mask = (mask ^ (mask << j)) & 0xFFFFFFFF
    return words


def _kth_largest(count_ge, topk, shape):
    zero = jnp.zeros(shape, I32)
    c = count_ge(zero)
    ok = c >= topk
    tau = jnp.where(ok, zero, jnp.full(shape, INT_MIN, I32))
    n_ge = jnp.where(ok, c, 0)

    def step(it, carry):
        tau, n_ge = carry
        cand = tau | lax.shift_left(jnp.int32(1), 30 - it)
        c = count_ge(cand)
        ok = c >= topk
        return jnp.where(ok, cand, tau), jnp.where(ok, c, n_ge)

    return lax.fori_loop(0, 31, step, (tau, n_ge))


def _dsa_prompt_kernel(qx_ref, qc_ref, wi_ref, kct_ref, kt_ref, v_ref, tri_ref, sel_ref, o_ref,
                       keys_ref, planes_ref, alive_ref, aq_ref, ac_ref, wb_ref, m_ref, l_ref, acc_ref, lg_ref, p_ref, bias_ref,
                       *, tq, ck, topk):
    i = pl.program_id(1)
    row0 = i * tq
    n_chunks = (row0 + tq + ck - 1) // ck
    n_sub = ck // LANES
    rows_q = N_HEADS * tq

    for h in range(IDX_HEADS):
        ac_ref[h * tq:(h + 1) * tq, :] = qc_ref[0, :, h * MXU_DIM:(h + 1) * MXU_DIM]
        aq_ref[h * tq:(h + 1) * tq, :] = qx_ref[0, :, h * MXU_DIM:(h + 1) * MXU_DIM]
        wb_ref[h] = jnp.broadcast_to(wi_ref[0, :, h:h + 1], (tq, LANES))

    row_id = row0 + lax.broadcasted_iota(I32, (tq, ck), 0)
    col_id = lax.broadcasted_iota(I32, (tq, ck), 1)

    def score_body(c, carry):
        col0 = pl.multiple_of(c * ck, ck)
        s = _dot(ac_ref[...], kct_ref[0, :, pl.ds(col0, ck)])
        tot = jnp.zeros((tq, ck), F32)
        for h in range(IDX_HEADS):
            tot = tot + jnp.maximum(s[h * tq:(h + 1) * tq, :], 0.0) * jnp.tile(wb_ref[h], (1, n_sub))
        key = jnp.where(col0 + col_id <= row_id, _sortable_key(tot), INT_MIN)
        keys_ref[:, pl.ds(col0, ck)] = key
        u = key.T ^ INT_MIN
        for blk in range(ck // PLANE_KEYS):
            slabs = [u[blk * PLANE_KEYS + 8 * k:blk * PLANE_KEYS + 8 * k + 8, :] for k in range(32)]
            slabs = _bit_transpose32(slabs)
            g0 = pl.multiple_of(c * (ck // 32) + blk * 8, 8)
            for o in range(32):
                planes_ref[o, pl.ds(g0, 8), :] = slabs[o]
        return carry

    @pl.when((pl.program_id(0) == 0) & (i == 0))
    def _():
        planes_ref[...] = jnp.zeros(planes_ref.shape, I32)

    lax.fori_loop(0, n_chunks, score_body, 0)

    n_groups = planes_ref.shape[1]
    group_id = lax.broadcasted_iota(I32, (n_groups, tq), 0)
    alive_ref[...] = jnp.where(group_id < n_chunks * (ck // 32), -1, 0)

    def ones_per_row(words):
        cnt = lax.population_count(words).reshape(n_groups // 8, 8, tq)
        return jnp.sum(jnp.sum(cnt, axis=0), axis=0, keepdims=True)

    def bit_step(it, carry):
        tau_u, n_above = carry
        alive = alive_ref[...]
        ones = alive & planes_ref[it]
        c1 = ones_per_row(ones)
        accept = (n_above + c1) >= topk
        alive_ref[...] = jnp.where(accept, ones, alive ^ ones)
        bit = lax.shift_left(jnp.int32(1), 31 - it)
        return jnp.where(accept, tau_u | bit, tau_u), jnp.where(accept, n_above, n_above + c1)

    zero = jnp.zeros((1, tq), I32)
    tau_u, n_gt = lax.fori_loop(0, 32, bit_step, (zero, zero))
    tau = tau_u ^ INT_MIN
    n_ge = jnp.where(tau > INT_MIN, n_gt + ones_per_row(alive_ref[...]), 0)

    def per_row(x):
        return jnp.broadcast_to(x, (LANES, tq)).T

    tau_lanes = jnp.maximum(tau, INT_MIN + 1)
    tau_sel = per_row(tau_lanes)
    has_ties = jnp.max(n_ge) > topk

    m_ref[...] = jnp.full(m_ref.shape, MASKED_MAX, F32)
    l_ref[...] = jnp.zeros(l_ref.shape, F32)
    acc_ref[...] = jnp.zeros(acc_ref.shape, F32)

    def attend(ranked):
        tau_t = jnp.tile(tau_sel, (1, n_sub))
        if ranked:
            need_eq = per_row((topk - n_gt).astype(F32))[:, 0:1]

        def body(c, eq_seen):
            col0 = pl.multiple_of(c * ck, ck)
            blk = keys_ref[:, pl.ds(col0, ck)]
            if ranked:
                eq = blk == tau_t
                eq_f = jnp.where(eq, 1.0, 0.0)
                before = _dot(eq_f.astype(BF16), tri_ref[...]) + eq_seen
                rank = jnp.where(eq, before, jnp.where(blk > tau_t, -1.0, jnp.inf))
                bias_ref[...] = jnp.where(rank < need_eq, 0.0, -jnp.inf)
                eq_seen = eq_seen + jnp.sum(eq_f, axis=1, keepdims=True)
            else:
                bias_ref[...] = jnp.where(blk >= tau_t, 0.0, -jnp.inf)
            for g in range(rows_q // ATT_ROWS):
                rows = slice(g * ATT_ROWS, (g + 1) * ATT_ROWS)
                off = (g * ATT_ROWS) % tq
                lg_ref[rows, :] = (_dot(aq_ref[rows, :], kt_ref[0, :, pl.ds(col0, ck)])
                                   + bias_ref[off:off + ATT_ROWS, :])
                m_old = m_ref[rows, :]
                m_new = jnp.maximum(m_old, jnp.max(lg_ref[rows, :], axis=1, keepdims=True))
                m_ref[rows, :] = m_new
                alpha = jnp.exp2(m_old - m_new)
                p = jnp.exp2(lg_ref[rows, :] - jnp.tile(m_new, (1, n_sub)))
                p_ref[rows, :] = p.astype(BF16)
                l_ref[rows, :] = alpha * l_ref[rows, :] + jnp.sum(p, axis=1, keepdims=True)
                acc_ref[rows, :] = (jnp.tile(alpha, (1, KV_WIDTH // LANES)) * acc_ref[rows, :]
                                    + _dot(p_ref[rows, :], v_ref[0, pl.ds(col0, ck), :]))
            return eq_seen
        lax.fori_loop(0, n_chunks, body, jnp.zeros((tq, 1), F32))

    @pl.when(has_ties)
    def _():
        attend(True)

    @pl.when(jnp.logical_not(has_ties))
    def _():
        attend(False)

    out = jnp.zeros((tq, ATT_WIDTH), F32)
    for h in range(N_HEADS):
        rows = slice(h * tq, (h + 1) * tq)
        o_h = acc_ref[rows, :] / jnp.tile(l_ref[rows, :], (1, KV_WIDTH // LANES))
        out = out + _dot(o_h.astype(BF16), sel_ref[h])
    o_ref[0] = out.astype(BF16)


def _dsa_prompt(qx, qc, wi, kct, kt, vb, sel):
    b, t, _ = qx.shape
    tq, ck = min(Q_TILE, t), min(K_CHUNK, t)
    assert t % tq == 0 and t % ck == 0 and ck % tq == 0
    topk = min(TOPK_MAX, t // 4)
    tri = jnp.asarray(np.triu(np.ones((ck, ck), np.float32), 1), BF16)
    tile = lambda width: pl.BlockSpec((1, tq, width), lambda i, j: (i, j, 0))
    whole = lambda rows, cols: pl.BlockSpec((1, rows, cols), lambda i, j: (i, 0, 0),
                                            pipeline_mode=pl.Buffered(1))
    return pl.pallas_call(
        functools.partial(_dsa_prompt_kernel, tq=tq, ck=ck, topk=topk),
        grid=(b, t // tq),
        in_specs=[tile(N_HEADS * MXU_DIM), tile(IDX_HEADS * MXU_DIM), tile(LANES),
                  whole(MXU_DIM, t), whole(MXU_DIM, t), whole(t, KV_WIDTH),
                  _resident(tri.shape, 2), _resident(sel.shape, 2)],
        out_specs=tile(ATT_WIDTH),
        out_shape=jax.ShapeDtypeStruct((b, t, ATT_WIDTH), BF16),
        scratch_shapes=[pltpu.VMEM((tq, t), I32),
                        pltpu.VMEM((32, t // 32, tq), I32),
                        pltpu.VMEM((t // 32, tq), I32),
                        pltpu.VMEM((N_HEADS * tq, MXU_DIM), BF16),
                        pltpu.VMEM((IDX_HEADS * tq, MXU_DIM), BF16),
                        pltpu.VMEM((IDX_HEADS, tq, LANES), F32),
                        pltpu.VMEM((N_HEADS * tq, LANES), F32),
                        pltpu.VMEM((N_HEADS * tq, LANES), F32),
                        pltpu.VMEM((N_HEADS * tq, KV_WIDTH), F32),
                        pltpu.VMEM((N_HEADS * tq, ck), F32),
                        pltpu.VMEM((N_HEADS * tq, ck), BF16),
                        pltpu.VMEM((tq, ck), F32)],
        compiler_params=_params(2),
        name="dsa_prompt",
    )(qx, qc, wi, kct, kt, vb, tri, sel)


def _dsa_sample_kernel(pt_ref, qx_ref, qc_ref, wb_ref, kcn_ref, kn_ref, vn_ref, tri_ref,
                       cik_hbm, ck_hbm, cv_hbm, o_ref,
                       ik_buf, k_buf, v_buf, keys_ref, bias_ref, lg_ref, sem,
                       *, layer, n_pages, page, chunk, tn, topk):
    b = pl.program_id(0)
    nb = pl.num_programs(0)
    past = n_pages * page
    width = past + LANES
    rows = tn * N_HEADS

    def copies(sample, slot, p):
        pg = pt_ref[sample, p]
        dst = pl.ds(pl.multiple_of(p * page, page), page)
        return (pltpu.make_async_copy(cik_hbm.at[layer, pg], ik_buf.at[slot, :, dst], sem.at[slot, 0]),
                pltpu.make_async_copy(ck_hbm.at[layer, pg], k_buf.at[slot, :, dst], sem.at[slot, 1]),
                pltpu.make_async_copy(cv_hbm.at[layer, pg], v_buf.at[slot, :, dst], sem.at[slot, 2]))

    def fetch(sample, slot):
        def body(p, carry):
            for cp in copies(sample, slot, p):
                cp.start()
            return carry
        lax.fori_loop(0, n_pages, body, 0)

    def wait(sample, slot):
        def body(p, carry):
            for cp in copies(sample, slot, p):
                cp.wait()
            return carry
        lax.fori_loop(0, n_pages, body, 0)

    slot = b % 2

    @pl.when(b == 0)
    def _():
        fetch(0, 0)

    @pl.when(b + 1 < nb)
    def _():
        fetch(b + 1, 1 - slot)

    wait(b, slot)

    qc = qc_ref[0]
    qx = qx_ref[0]
    wb = wb_ref[0]
    tok = lax.broadcasted_iota(I32, (tn, LANES), 0)
    col = lax.broadcasted_iota(I32, (tn, LANES), 1)

    def head_sum(s):
        n = s.shape[1]
        r = jnp.maximum(s, 0.0) * jnp.tile(wb, (1, n // LANES))
        return jnp.sum(r.reshape(tn, N_HEADS, n), axis=1)

    keys_ref[...] = jnp.full(keys_ref.shape, INT_MIN, I32)

    zero_rows = jnp.zeros((MXU_DIM - 3 * IDX_DIM, chunk), BF16)
    for c in range(past // chunk):
        cols = slice(c * chunk, (c + 1) * chunk)
        hi, lo = _split2(ik_buf[slot, :, cols])
        kct = jnp.concatenate([hi, hi, lo, zero_rows], axis=0)
        keys_ref[0:tn, cols] = _sortable_key(head_sum(_dot(qc, kct)))
    s_new = head_sum(_dot(qc, kcn_ref[0]))
    keys_ref[0:tn, past:width] = jnp.where(col <= tok, _sortable_key(s_new), INT_MIN)

    def count_ge(cand):
        hit = jnp.where(keys_ref[...] >= cand, 1, 0)
        parts = [hit[:, u * LANES:(u + 1) * LANES] for u in range(width // LANES)]
        while len(parts) > 1:
            parts = [a + b for a, b in zip(parts[0::2], parts[1::2])] + parts[len(parts) & ~1:]
        return jnp.sum(parts[0], axis=1, keepdims=True)

    tau, n_ge = _kth_largest(count_ge, topk, (8, 1))
    tau_sel = jnp.maximum(tau, INT_MIN + 1)
    has_ties = jnp.max(n_ge) > topk

    @pl.when(jnp.logical_not(has_ties))
    def _():
        bias_ref[...] = jnp.where(keys_ref[...] >= tau_sel, 0.0, -jnp.inf)

    @pl.when(has_ties)
    def _():
        n_gt = jnp.sum(jnp.where(keys_ref[...] > tau_sel, 1, 0), axis=1, keepdims=True)
        need_eq = (topk - n_gt).astype(F32)

        def body(u, eq_seen):
            c0 = pl.multiple_of(u * LANES, LANES)
            blk = keys_ref[:, pl.ds(c0, LANES)]
            eq = blk == tau_sel
            eq_f = jnp.where(eq, 1.0, 0.0)
            before = _dot(eq_f.astype(BF16), tri_ref[...]) + eq_seen
            rank = jnp.where(eq, before, jnp.where(blk > tau_sel, -1.0, jnp.inf))
            bias_ref[:, pl.ds(c0, LANES)] = jnp.where(rank < need_eq, 0.0, -jnp.inf)
            return eq_seen + jnp.sum(eq_f, axis=1, keepdims=True)

        lax.fori_loop(0, width // LANES, body, jnp.zeros((8, 1), F32))

    def expand(x):
        n = x.shape[1]
        return jnp.broadcast_to(x[0:tn].reshape(tn, 1, n), (tn, N_HEADS, n)).reshape(rows, n)

    for c in range(past // chunk):
        cols = slice(c * chunk, (c + 1) * chunk)
        lg_ref[:, cols] = _dot(qx, k_buf[slot, :, cols].astype(BF16)) + expand(bias_ref[:, cols])
    lg_ref[:, past:width] = _dot(qx, kn_ref[0]) + expand(bias_ref[:, past:width])
    lg = lg_ref[...]
    p = jnp.exp2(lg - jnp.max(lg, axis=1, keepdims=True))
    l = jnp.sum(p, axis=1, keepdims=True)
    lg_ref[...] = p
    acc = _dot_nt(lg_ref[:, past:width].astype(BF16), vn_ref[0])
    for c in range(past // chunk):
        cols = slice(c * chunk, (c + 1) * chunk)
        acc = acc + _dot_nt(lg_ref[:, cols].astype(BF16), v_buf[slot, :, cols].astype(BF16))
    o_ref[0] = acc / l


def _dsa_sample(page_table, qx, qc, wb, kcn, kn, vn, cache_ik, cache_k, cache_v, layer):
    db, rows, _ = qx.shape
    tn = rows // N_HEADS
    n_pages = page_table.shape[1]
    page = cache_ik.shape[3]
    past = n_pages * page
    chunk = min(SAMPLE_CHUNK, past)
    assert page == LANES and tn <= 8 and past % chunk == 0
    tri = jnp.asarray(np.triu(np.ones((page, page), np.float32), 1), BF16)
    topk = min(TOPK_MAX, (past + tn) // 4)
    per = lambda r, c: pl.BlockSpec((1, r, c), lambda i, pt: (i, 0, 0))
    grid_spec = pltpu.PrefetchScalarGridSpec(
        num_scalar_prefetch=1,
        grid=(db,),
        in_specs=[per(rows, MXU_DIM), per(rows, MXU_DIM), per(rows, LANES),
                  per(MXU_DIM, LANES), per(KV_WIDTH, LANES), per(KV_WIDTH, LANES),
                  pl.BlockSpec(tri.shape, lambda i, pt: (0, 0)),
                  pl.BlockSpec(memory_space=pl.ANY), pl.BlockSpec(memory_space=pl.ANY),
                  pl.BlockSpec(memory_space=pl.ANY)],
        out_specs=per(rows, KV_WIDTH),
        scratch_shapes=[pltpu.VMEM((2, IDX_DIM, past), F32),
                        pltpu.VMEM((2, KV_WIDTH, past), F32),
                        pltpu.VMEM((2, KV_WIDTH, past), F32),
                        pltpu.VMEM((8, past + LANES), I32),
                        pltpu.VMEM((8, past + LANES), F32),
                        pltpu.VMEM((rows, past + LANES), F32),
                        pltpu.SemaphoreType.DMA((2, 3))],
    )
    return pl.pallas_call(
        functools.partial(_dsa_sample_kernel, layer=layer, n_pages=n_pages, page=page, chunk=chunk,
                          tn=tn, topk=topk),
        grid_spec=grid_spec,
        out_shape=jax.ShapeDtypeStruct((db, rows, KV_WIDTH), F32),
        compiler_params=_params(1),
        name="dsa_sample",
    )(page_table, qx, qc, wb, kcn, kn, vn, tri, cache_ik, cache_k, cache_v)


def _merge_ffn_kernel(x_ref, att_ref, rec_ref, pg_ref, wa_ref, wh_ref, wo_ref,
                      g_ref, wg_ref, wu_ref, wd_ref, o_ref):
    d = x_ref.shape[1]
    pg = pg_ref[...]
    merged = (_sigmoid(pg[:, :d]) * _dot(att_ref[...], wa_ref[...])
              + _sigmoid(pg[:, d:]) * _dot(rec_ref[...], wh_ref[...]))
    x = x_ref[...] + _dot(merged.astype(BF16), wo_ref[...])
    o_ref[...] = _ffn_apply(x, g_ref, wg_ref, wu_ref, wd_ref)


def _merge_ffn(x, att, rec, pg, wa, wh, wo, norm, wg, wu, wd):
    n, d = x.shape
    tm = min(ROW_TILE, n)
    assert n % tm == 0 and wg.shape[1] % FFN_CHUNK == 0
    row = lambda width: pl.BlockSpec((tm, width), lambda i: (i, 0))
    weights = [wa, wh, wo, norm.reshape(1, d), wg, wu, wd]
    return pl.pallas_call(
        _merge_ffn_kernel,
        grid=(n // tm,),
        in_specs=[row(d), row(ATT_WIDTH), row(HG_WIDTH), row(2 * d)] + [_resident(a.shape, 1) for a in weights],
        out_specs=row(d),
        out_shape=jax.ShapeDtypeStruct((n, d), F32),
        compiler_params=_params(1),
        name="merge_ffn",
    )(x, att, rec, pg, *weights)


def _prep_layer(l, ffn1_norm, ffn1_w_gate, ffn1_w_up, ffn1_w_down, mix_norm, w_in, q_norm, k_norm, hg_norm,
                w_branch_attn, w_branch_hgrn, w_out, ffn2_norm, ffn2_w_gate, ffn2_w_up, ffn2_w_down):
    d = w_in.shape[1]
    w = w_in[l]
    off = [0]

    def take(width):
        a = w[:, off[0]:off[0] + width]
        off[0] += width
        return a

    wq, wk, wv = take(ATT_WIDTH), take(KV_WIDTH), take(KV_WIDTH)
    wqi, wki, wwi = take(IDX_HEADS * IDX_DIM), take(IDX_DIM), take(IDX_HEADS)
    wh = take(4 * HG_WIDTH)
    wg = take(2 * d)
    zeros = jnp.zeros((d, IDX_DIM), F32)
    wqi_heads = wqi.reshape(d, IDX_HEADS, IDX_DIM)
    wqi_arr = jnp.concatenate([wqi_heads, wqi_heads, wqi_heads, jnp.zeros_like(wqi_heads)], axis=2)
    wki_arr = jnp.concatenate([wki, wki, wki, zeros], axis=1)
    wwi_arr = jnp.concatenate([wwi, jnp.zeros((d, LANES - IDX_HEADS), F32)], axis=1)
    bd = np.kron(np.eye(ATT_WIDTH // HEAD_DIM, dtype=np.float32), np.ones((HEAD_DIM, HEAD_DIM), np.float32))
    pq = np.zeros((ATT_WIDTH, N_HEADS * MXU_DIM), np.float32)
    sel = np.zeros((N_HEADS, KV_WIDTH, ATT_WIDTH), np.float32)
    for h in range(N_HEADS):
        g = h // (N_HEADS // KV_HEADS)
        for e in range(HEAD_DIM):
            pq[h * HEAD_DIM + e, h * MXU_DIM + g * HEAD_DIM + e] = 1.0
            sel[h, g * HEAD_DIM + e, h * HEAD_DIM + e] = 1.0
    bf = lambda a: a.astype(BF16)
    return dict(
        ffn1=(ffn1_norm[l], bf(ffn1_w_gate[l]), bf(ffn1_w_up[l]), bf(ffn1_w_down[l])),
        ffn2=(ffn2_norm[l], bf(ffn2_w_gate[l]), bf(ffn2_w_up[l]), bf(ffn2_w_down[l])),
        proj=dict(mix_norm=mix_norm[l].reshape(1, d), wq=bf(wq), wk=bf(wk), wv=bf(wv),
                  wqi=bf(wqi_arr.reshape(d, IDX_HEADS * MXU_DIM)), wki=bf(wki_arr), wwi=bf(wwi_arr),
                  wh=bf(wh), wg=bf(wg), bd=jnp.asarray(bd, BF16), pq=jnp.asarray(pq, BF16),
                  qn=jnp.tile(q_norm[l], N_HEADS).reshape(1, ATT_WIDTH),
                  kn=jnp.tile(k_norm[l], KV_HEADS).reshape(1, KV_WIDTH)),
        sel=jnp.asarray(sel, BF16),
        hg_norm=hg_norm[l],
        merge=(bf(w_branch_attn[l]), bf(w_branch_hgrn[l]), bf(w_out[l])),
    )


def kernel(x_prompt, x_sample, cache_k, cache_v, cache_idx_k, state_hgrn, page_table, hg_lower_bound,
           ffn1_norm, ffn1_w_gate, ffn1_w_up, ffn1_w_down, mix_norm, w_in, q_norm, k_norm, hg_norm,
           w_branch_attn, w_branch_hgrn, w_out, ffn2_norm, ffn2_w_gate, ffn2_w_up, ffn2_w_down):
    b, t, d = x_prompt.shape
    db, tn, _ = x_sample.shape
    depth = w_in.shape[0]
    n_pool, page = cache_k.shape[1], cache_k.shape[2]
    past = page_table.shape[1] * page
    cache_ikt = cache_idx_k.transpose(0, 1, 3, 2)
    cache_kt = cache_k.transpose(0, 1, 3, 4, 2).reshape(depth, n_pool, KV_WIDTH, page)
    cache_vt = cache_v.transpose(0, 1, 3, 4, 2).reshape(depth, n_pool, KV_WIDTH, page)
    lb_param = hg_lower_bound.astype(F32)
    tn_pad = -(-tn // HG_SUB) * HG_SUB

    xp = x_prompt.reshape(b * t, d)
    xs = x_sample.reshape(db * tn, d)
    outs = [[] for _ in range(8)]
    for l in range(depth):
        w = _prep_layer(l, ffn1_norm, ffn1_w_gate, ffn1_w_up, ffn1_w_down, mix_norm, w_in, q_norm, k_norm,
                        hg_norm, w_branch_attn, w_branch_hgrn, w_out, ffn2_norm, ffn2_w_gate, ffn2_w_up,
                        ffn2_w_down)
        xp = _ffn(xp, *w["ffn1"])
        qx, k, v, vb, qc, ki, wi, ph, pg, kt, kct = _proj(xp, b, t, 0, w["proj"], transposed=True)
        att = _dsa_prompt(qx.reshape(b, t, -1), qc.reshape(b, t, -1), wi.reshape(b, t, -1),
                          kct, kt, vb.reshape(b, t, -1), w["sel"])
        rec, s_p = _hgrn(ph.reshape(b, t, -1), lb_param, jnp.zeros((b, HG_HEADS, HG_KDIM, HG_VDIM), F32),
                         w["hg_norm"], l, t)
        xp = _merge_ffn(xp, att.reshape(b * t, -1), rec.reshape(b * t, -1), pg, *w["merge"], *w["ffn2"])
        outs[0].append(k.reshape(b, t, KV_HEADS, HEAD_DIM))
        outs[1].append(v.reshape(b, t, KV_HEADS, HEAD_DIM))
        outs[2].append(ki.reshape(b, t, IDX_DIM))
        outs[3].append(s_p.astype(x_prompt.dtype))

        xs = _ffn(xs, *w["ffn1"])
        qx, k, v, vb, qc, ki, wi, ph, pg, kb, kc = _proj(xs, db, tn, past, w["proj"], transposed=False)
        rows = tn * N_HEADS
        qx3 = qx.reshape(db, rows, MXU_DIM)
        qc3 = qc.reshape(db, rows, MXU_DIM)
        wb3 = jnp.broadcast_to(wi[:, :IDX_HEADS].reshape(db, rows, 1), (db, rows, LANES))
        pad_new = lambda a: jnp.pad(a.reshape(db, tn, -1), ((0, 0), (0, page - tn), (0, 0))).transpose(0, 2, 1)
        o_s = _dsa_sample(page_table, qx3, qc3, wb3, pad_new(kc), pad_new(kb), pad_new(vb),
                          cache_ikt, cache_kt, cache_vt, l)
        o_s = o_s.reshape(db, tn, KV_HEADS, N_HEADS // KV_HEADS, KV_HEADS, HEAD_DIM)
        att_s = jnp.stack([o_s[:, :, g, :, g, :] for g in range(KV_HEADS)], axis=2)
        att_s = att_s.reshape(db * tn, ATT_WIDTH).astype(BF16)
        ph3 = jnp.pad(ph.reshape(db, tn, -1), ((0, 0), (0, tn_pad - tn), (0, 0)))
        rec, s_s = _hgrn(ph3, lb_param, state_hgrn[l], w["hg_norm"], l, tn)
        rec = rec[:, :tn].reshape(db * tn, HG_WIDTH)
        xs = _merge_ffn(xs, att_s, rec, pg, *w["merge"], *w["ffn2"])
        outs[4].append(k.reshape(db, tn, KV_HEADS, HEAD_DIM))
        outs[5].append(v.reshape(db, tn, KV_HEADS, HEAD_DIM))
        outs[6].append(ki.reshape(db, tn, IDX_DIM))
        outs[7].append(s_s.astype(state_hgrn.dtype))

    return (xp.reshape(b, t, d), xs.reshape(db, tn, d)) + tuple(jnp.stack(o) for o in outs)
```

```python
import functools

import numpy as np
import jax
import jax.numpy as jnp
from jax import lax
from jax.experimental import pallas as pl
from jax.experimental.pallas import tpu as pltpu

F32, BF16, I32 = jnp.float32, jnp.bfloat16, jnp.int32

N_HEADS, KV_HEADS, HEAD_DIM = 8, 4, 64
IDX_HEADS, IDX_DIM, TOPK_MAX = 8, 64, 256
HG_HEADS, HG_KDIM, HG_VDIM = 4, 128, 128
ATT_WIDTH = N_HEADS * HEAD_DIM
KV_WIDTH = KV_HEADS * HEAD_DIM
HG_WIDTH = HG_HEADS * HG_KDIM
ROPE_THETA = 10000.0
NORM_EPS = 1e-6

LANES = 128
MXU_DIM = 256
VMEM_LIMIT_BYTES = 56 * 2**20
INT_MIN = -2**31
MASKED_MAX = -1e30
LOG2E = 1.4426950408889634

ROW_TILE = 512
PROJ_ROW_TILE = 256
FFN_CHUNK = 256
HG_STEP = 256
HG_CHUNK = 64
HG_GROUP = 2
HG_SAFE_DECAY = 80.0
HG_SUB = 16
Q_TILE = 256
K_CHUNK = 512
ATT_ROWS = 256
PLANE_KEYS = 256
SAMPLE_CHUNK = 2048


def _params(n_axes):
    return pltpu.CompilerParams(dimension_semantics=("arbitrary",) * n_axes,
                                vmem_limit_bytes=VMEM_LIMIT_BYTES)


def _resident(shape, n_axes):
    zeros = (0,) * len(shape)
    if n_axes == 1:
        return pl.BlockSpec(shape, lambda i: zeros, pipeline_mode=pl.Buffered(1))
    return pl.BlockSpec(shape, lambda i, j: zeros, pipeline_mode=pl.Buffered(1))


def _rms(x, gain):
    return x * lax.rsqrt(jnp.mean(x * x, axis=-1, keepdims=True) + NORM_EPS) * gain


def _sigmoid(x):
    return 1.0 / (1.0 + jnp.exp(-x))


def _dot(a, b):
    return jnp.dot(a, b, preferred_element_type=F32)


def _dot_nt(a, b):
    return lax.dot_general(a, b, (((1,), (1,)), ((), ())), preferred_element_type=F32)


def _dot_tn(a, b):
    return lax.dot_general(a, b, (((0,), (0,)), ((), ())), preferred_element_type=F32)


def _split2(x):
    hi = x.astype(BF16)
    return hi, (x - hi.astype(F32)).astype(BF16)


def _ffn_apply(x, g_ref, wg_ref, wu_ref, wd_ref):
    h = _rms(x, g_ref[...]).astype(BF16)
    d_ff = wg_ref.shape[1]
    acc = jnp.zeros(x.shape, F32)
    for c in range(d_ff // FFN_CHUNK):
        cols = slice(c * FFN_CHUNK, (c + 1) * FFN_CHUNK)
        g = _dot(h, wg_ref[:, cols])
        u = _dot(h, wu_ref[:, cols])
        a = (g * _sigmoid(g) * u).astype(BF16)
        acc = acc + _dot(a, wd_ref[cols, :])
    return x + 0.5 * acc


def _ffn_kernel(x_ref, g_ref, wg_ref, wu_ref, wd_ref, o_ref):
    o_ref[...] = _ffn_apply(x_ref[...], g_ref, wg_ref, wu_ref, wd_ref)


def _ffn(x, norm, wg, wu, wd):
    n, d = x.shape
    tm = min(ROW_TILE, n)
    assert n % tm == 0 and wg.shape[1] % FFN_CHUNK == 0
    return pl.pallas_call(
        _ffn_kernel,
        grid=(n // tm,),
        in_specs=[pl.BlockSpec((tm, d), lambda i: (i, 0)), _resident((1, d), 1),
                  _resident(wg.shape, 1), _resident(wu.shape, 1), _resident(wd.shape, 1)],
        out_specs=pl.BlockSpec((tm, d), lambda i: (i, 0)),
        out_shape=jax.ShapeDtypeStruct((n, d), F32),
        compiler_params=_params(1),
        name="ffn",
    )(x, norm.reshape(1, d), wg, wu, wd)


def _rope(y, cos, sin):
    n = y.shape[1]
    reps = n // LANES
    c = jnp.tile(cos, (1, reps)) if reps > 1 else cos
    s = jnp.tile(sin, (1, reps)) if reps > 1 else sin
    lane = lax.broadcasted_iota(I32, y.shape, 1)
    first_half = (lane & (HEAD_DIM // 2)) == 0
    partner = jnp.where(first_half, pltpu.roll(y, n - HEAD_DIM // 2, axis=1),
                        pltpu.roll(y, HEAD_DIM // 2, axis=1))
    return y * c + partner * s


def _proj_kernel(x_ref, gn_ref, wq_ref, wk_ref, wv_ref, wqi_ref, wki_ref, wwi_ref, wh_ref, wg_ref,
                 bd_ref, pq_ref, qn_ref, kn_ref, cos_ref, sin_ref,
                 qx_ref, k_ref, v_ref, vb_ref, qc_ref, ki_ref, wi_ref, ph_ref, pg_ref, kb_ref, kc_ref,
                 *, transposed):
    h = _rms(x_ref[...], gn_ref[...]).astype(BF16)
    cos, sin = cos_ref[...], sin_ref[...]

    def head_norm(y, gain):
        n = y.shape[1]
        hi, lo = _split2(y * y)
        bd = bd_ref[:n, :n]
        ss = _dot(hi, bd) + _dot(lo, bd)
        return y * lax.rsqrt(ss * (1.0 / HEAD_DIM) + NORM_EPS) * gain

    q = _rope(head_norm(_dot(h, wq_ref[...]), qn_ref[...]), cos, sin) * (LOG2E * HEAD_DIM ** -0.5)
    qx_ref[...] = _dot(q.astype(BF16), pq_ref[...]).astype(BF16)

    k = _rope(head_norm(_dot(h, wk_ref[...]), kn_ref[...]), cos, sin)
    k_ref[...] = k
    v = _dot(h, wv_ref[...])
    v_ref[...] = v
    vb_ref[...] = v.astype(BF16)

    qi = _rope(_dot(h, wqi_ref[...]), cos, sin) * (IDX_DIM ** -0.5)
    lane = lax.broadcasted_iota(I32, qi.shape, 1) & (MXU_DIM - 1)
    is_lo = (lane >= IDX_DIM) & (lane < 2 * IDX_DIM)
    qc_ref[...] = jnp.where(is_lo, qi - qi.astype(BF16).astype(F32), qi).astype(BF16)

    ki = _rope(_dot(h, wki_ref[...]), cos, sin)
    ki_ref[...] = ki[:, :IDX_DIM]
    lane = lax.broadcasted_iota(I32, ki.shape, 1)
    is_lo = (lane >= 2 * IDX_DIM) & (lane < 3 * IDX_DIM)
    kc = jnp.where(is_lo, ki - ki.astype(BF16).astype(F32), ki)

    if transposed:
        kb_ref[0] = k.T.astype(BF16)
        kc_ref[0] = kc.T.astype(BF16)
    else:
        kb_ref[...] = k.astype(BF16)
        kc_ref[...] = kc.astype(BF16)

    wi_ref[...] = _dot(h, wwi_ref[...]) * (IDX_HEADS ** -0.5)
    ph_ref[...] = _dot(h, wh_ref[...])
    pg_ref[...] = _dot(h, wg_ref[...])


def _proj(x, batch, seq, pos0, w, transposed):
    n, d = x.shape
    tm = min(PROJ_ROW_TILE, n)
    assert n % tm == 0
    lane = np.arange(LANES)
    inv = ROPE_THETA ** (-(lane % (HEAD_DIM // 2)).astype(np.float32) * 2.0 / HEAD_DIM)
    sign = np.where((lane % HEAD_DIM) < HEAD_DIM // 2, -1.0, 1.0).astype(np.float32)
    ang = (pos0 + jnp.arange(seq)).astype(F32)[:, None] * jnp.asarray(inv)[None, :]
    cos_t, sin_t = jnp.cos(ang), jnp.sin(ang) * jnp.asarray(sign)[None, :]
    if seq >= tm:
        assert seq % tm == 0
        per_seq = seq // tm
        tab_map = lambda i: (i % per_seq, 0)
    else:
        assert tm % seq == 0
        cos_t, sin_t = jnp.tile(cos_t, (tm // seq, 1)), jnp.tile(sin_t, (tm // seq, 1))
        tab_map = lambda i: (0, 0)

    row = lambda width: pl.BlockSpec((tm, width), lambda i: (i, 0))
    if transposed:
        per_seq = seq // tm
        kt_spec = pl.BlockSpec((1, MXU_DIM, tm), lambda i: (i // per_seq, 0, i % per_seq))
        kt_shape = jax.ShapeDtypeStruct((batch, MXU_DIM, seq), BF16)
    else:
        kt_spec = row(MXU_DIM)
        kt_shape = jax.ShapeDtypeStruct((n, MXU_DIM), BF16)

    weights = [w["wq"], w["wk"], w["wv"], w["wqi"], w["wki"], w["wwi"], w["wh"], w["wg"],
               w["bd"], w["pq"], w["qn"], w["kn"]]
    out_shape = [
        jax.ShapeDtypeStruct((n, N_HEADS * MXU_DIM), BF16),
        jax.ShapeDtypeStruct((n, KV_WIDTH), F32),
        jax.ShapeDtypeStruct((n, KV_WIDTH), F32),
        jax.ShapeDtypeStruct((n, KV_WIDTH), BF16),
        jax.ShapeDtypeStruct((n, IDX_HEADS * MXU_DIM), BF16),
        jax.ShapeDtypeStruct((n, IDX_DIM), F32),
        jax.ShapeDtypeStruct((n, LANES), F32),
        jax.ShapeDtypeStruct((n, 4 * HG_WIDTH), F32),
        jax.ShapeDtypeStruct((n, 2 * d), F32),
        kt_shape,
        kt_shape,
    ]
    out_specs = [row(N_HEADS * MXU_DIM), row(KV_WIDTH), row(KV_WIDTH), row(KV_WIDTH),
                 row(IDX_HEADS * MXU_DIM), row(IDX_DIM), row(LANES), row(4 * HG_WIDTH), row(2 * d),
                 kt_spec, kt_spec]
    return pl.pallas_call(
        functools.partial(_proj_kernel, transposed=transposed),
        grid=(n // tm,),
        in_specs=[row(d), _resident((1, d), 1)] + [_resident(a.shape, 1) for a in weights]
                 + [pl.BlockSpec((tm, LANES), tab_map), pl.BlockSpec((tm, LANES), tab_map)],
        out_specs=out_specs,
        out_shape=out_shape,
        compiler_params=_params(1),
        name="proj",
    )(x, w["mix_norm"], *weights, cos_t, sin_t)


def _hgrn_kernel(ph_ref, lbp_ref, s0_ref, gn_ref, tri_ref, rec_ref, s_ref, st_ref, o_ref,
                 *, layer, step, chunk, sub, t_valid):
    j = pl.program_id(1)

    @pl.when(j == 0)
    def _():
        for hh in range(HG_HEADS):
            st_ref[hh] = s0_ref[0, hh].T

    lbp = lbp_ref[...]
    e = jnp.exp(lbp - jnp.max(lbp, axis=0, keepdims=True))
    lb_all = jnp.sum(e[:layer + 1], axis=0, keepdims=True) / jnp.sum(e, axis=0, keepdims=True)

    tri = tri_ref[...]
    row_c = lax.broadcasted_iota(I32, (chunk, HG_KDIM), 0)
    row_s = lax.broadcasted_iota(I32, (sub, HG_KDIM), 0)
    causal = (lax.broadcasted_iota(I32, (chunk, chunk), 0) >= lax.broadcasted_iota(I32, (chunk, chunk), 1))
    w = HG_WIDTH

    group = HG_GROUP if (step // chunk) % HG_GROUP == 0 else 1

    def chunk_body(ci, carry):
        chunks = []
        worst = jnp.float32(0.0)
        for cc in range(group):
            r0 = pl.multiple_of((ci * group + cc) * chunk, chunk)
            rows = pl.ds(r0, chunk)
            valid = (j * step + r0 + row_c) < t_valid
            heads = []
            for hh in range(HG_HEADS):
                lanes = slice(hh * HG_KDIM, (hh + 1) * HG_KDIM)
                hq = ph_ref[0, rows, hh * HG_KDIM:(hh + 1) * HG_KDIM]
                z = ph_ref[0, rows, w + hh * HG_KDIM:w + (hh + 1) * HG_KDIM]
                v = ph_ref[0, rows, 2 * w + hh * HG_VDIM:2 * w + (hh + 1) * HG_VDIM]
                hg = ph_ref[0, rows, 3 * w + hh * HG_VDIM:3 * w + (hh + 1) * HG_VDIM]
                lb = lb_all[:, lanes]
                q = hq * _sigmoid(hq) * (HG_KDIM ** -0.5)
                logf = jnp.where(valid, jnp.log(lb + (1.0 - lb) * _sigmoid(z)), 0.0)
                kk = jnp.where(valid, (1.0 - lb) * _sigmoid(-z), 0.0)
                l1 = logf.astype(BF16)
                r1 = logf - l1.astype(F32)
                l2 = r1.astype(BF16)
                l3 = (r1 - l2.astype(F32)).astype(BF16)
                g = _dot(tri, l1) + _dot(tri, l2) + _dot(tri, l3)
                heads.append((q, kk, v, hg, g))
                worst = jnp.maximum(worst, jnp.max(-g[chunk - 1:chunk, :]))
            chunks.append((rows, heads))

        def finish(rows, hh, o, kk, v, hg, g):
            g_last = g[chunk - 1:chunk, :]
            k_end = (kk * jnp.exp(g_last - g)).astype(BF16)
            st_ref[hh] = jnp.exp(g_last) * st_ref[hh] + _dot_tn(v.astype(BF16), k_end)
            gate = hg * _sigmoid(hg)
            rec_ref[0, rows, hh * HG_VDIM:(hh + 1) * HG_VDIM] = (_rms(o, gn_ref[...]) * gate).astype(BF16)

        @pl.when(worst <= HG_SAFE_DECAY)
        def _():
            for rows, heads in chunks:
                for hh, (q, kk, v, hg, g) in enumerate(heads):
                    qg = (q * jnp.exp(g)).astype(BF16)
                    a = jnp.where(causal, _dot_nt(qg, (kk * jnp.exp(-g)).astype(BF16)), 0.0)
                    o = _dot_nt(qg, st_ref[hh].astype(BF16)) + _dot(a.astype(BF16), v.astype(BF16))
                    finish(rows, hh, o, kk, v, hg, g)

        @pl.when(worst > HG_SAFE_DECAY)
        def _():
            for rows, heads in chunks:
                for hh, (q, kk, v, hg, g) in enumerate(heads):
                    vb = v.astype(BF16)
                    o_ref[hh] = _dot_nt((q * jnp.exp(g)).astype(BF16), st_ref[hh].astype(BF16))
                    for i in range(chunk // sub):
                        b0 = i * sub
                        gi, qi, ki_, vi = g[b0:b0 + sub], q[b0:b0 + sub], kk[b0:b0 + sub], v[b0:b0 + sub]
                        if i > 0:
                            ref = g[b0 - 1:b0, :]
                            q_rel = (qi * jnp.exp(gi - ref)).astype(BF16)
                            k_rel = (kk[:b0] * jnp.exp(ref - g[:b0])).astype(BF16)
                            a = _dot_nt(q_rel, k_rel)
                            o_ref[hh, b0:b0 + sub, :] += _dot(a.astype(BF16), vb[:b0])
                        for t in range(sub):
                            dec = jnp.exp(jnp.where(row_s <= t, gi[t:t + 1, :] - gi, -jnp.inf))
                            a_t = jnp.sum(qi[t:t + 1, :] * dec * ki_, axis=1, keepdims=True)
                            o_ref[hh, b0 + t:b0 + t + 1, :] += jnp.sum(a_t * vi, axis=0, keepdims=True)
                    finish(rows, hh, o_ref[hh], kk, v, hg, g)
        return carry

    lax.fori_loop(0, step // chunk // group, chunk_body, 0)

    @pl.when(j == pl.num_programs(1) - 1)
    def _():
        for hh in range(HG_HEADS):
            s_ref[0, hh] = st_ref[hh].T


def _hgrn(ph, lb_param, s0, hg_norm, layer, t_valid):
    b, t, _ = ph.shape
    step = min(HG_STEP, t)
    chunk = min(HG_CHUNK, step)
    sub = min(HG_SUB, chunk)
    assert t % step == 0 and step % chunk == 0 and chunk % sub == 0
    tri = jnp.asarray(np.tril(np.ones((chunk, chunk), np.float32)), BF16)
    return pl.pallas_call(
        functools.partial(_hgrn_kernel, layer=layer, step=step, chunk=chunk, sub=sub, t_valid=t_valid),
        grid=(b, t // step),
        in_specs=[pl.BlockSpec((1, step, 4 * HG_WIDTH), lambda i, j: (i, j, 0)),
                  _resident(lb_param.shape, 2),
                  pl.BlockSpec((1, HG_HEADS, HG_KDIM, HG_VDIM), lambda i, j: (i, 0, 0, 0)),
                  _resident((1, HG_VDIM), 2), _resident(tri.shape, 2)],
        out_specs=[pl.BlockSpec((1, step, HG_WIDTH), lambda i, j: (i, j, 0)),
                   pl.BlockSpec((1, HG_HEADS, HG_KDIM, HG_VDIM), lambda i, j: (i, 0, 0, 0))],
        out_shape=[jax.ShapeDtypeStruct((b, t, HG_WIDTH), BF16),
                   jax.ShapeDtypeStruct((b, HG_HEADS, HG_KDIM, HG_VDIM), F32)],
        scratch_shapes=[pltpu.VMEM((HG_HEADS, HG_VDIM, HG_KDIM), F32),
                        pltpu.VMEM((HG_HEADS, chunk, HG_VDIM), F32)],
        compiler_params=_params(2),
        name="hgrn",
    )(ph, lb_param, s0, hg_norm.reshape(1, HG_VDIM), tri)


def _sortable_key(score):
    bits = pltpu.bitcast(score, I32)
    return jnp.where(bits < 0, INT_MIN - bits, bits)


def _bit_transpose32(words):
    words = list(words)
    j, mask = 16, 0x0000FFFF
    while j:
        for k in range(32):
            if k & j == 0:
                t = (words[k] ^ lax.shift_right_logical(words[k + j], jnp.int32(j))) & mask
                words[k] = words[k] ^ t
                words[k + j] = words[k + j] ^ lax.shift_left(t, jnp.int32(j))
        j >>= 1
        mask = (mask ^ (mask << j)) & 0xFFFFFFFF
    return words


def _kth_largest(count_ge, topk, shape):
    zero = jnp.zeros(shape, I32)
    c = count_ge(zero)
    ok = c >= topk
    tau = jnp.where(ok, zero, jnp.full(shape, INT_MIN, I32))
    n_ge = jnp.where(ok, c, 0)

    def step(it, carry):
        tau, n_ge = carry
        cand = tau | lax.shift_left(jnp.int32(1), 30 - it)
        c = count_ge(cand)
        ok = c >= topk
        return jnp.where(ok, cand, tau), jnp.where(ok, c, n_ge)

    return lax.fori_loop(0, 31, step, (tau, n_ge))


def _dsa_prompt_kernel(qx_ref, qc_ref, wi_ref, kct_ref, kt_ref, v_ref, tri_ref, sel_ref, o_ref,
                       keys_ref, planes_ref, alive_ref, aq_ref, ac_ref, wb_ref, m_ref, l_ref, acc_ref, lg_ref, p_ref, bias_ref,
                       *, tq, ck, topk):
    i = pl.program_id(1)
    row0 = i * tq
    n_chunks = (row0 + tq + ck - 1) // ck
    n_sub = ck // LANES
    rows_q = N_HEADS * tq

    for h in range(IDX_HEADS):
        ac_ref[h * tq:(h + 1) * tq, :] = qc_ref[0, :, h * MXU_DIM:(h + 1) * MXU_DIM]
        aq_ref[h * tq:(h + 1) * tq, :] = qx_ref[0, :, h * MXU_DIM:(h + 1) * MXU_DIM]
        wb_ref[h] = jnp.broadcast_to(wi_ref[0, :, h:h + 1], (tq, LANES))

    row_id = row0 + lax.broadcasted_iota(I32, (tq, ck), 0)
    col_id = lax.broadcasted_iota(I32, (tq, ck), 1)

    def score_body(c, carry, diagonal=True):
        col0 = pl.multiple_of(c * ck, ck)
        s = _dot(ac_ref[...], kct_ref[0, :, pl.ds(col0, ck)])
        tot = jnp.zeros((tq, ck), F32)
        for h in range(IDX_HEADS):
            tot = tot + jnp.maximum(s[h * tq:(h + 1) * tq, :], 0.0) * jnp.tile(wb_ref[h], (1, n_sub))
        key = _sortable_key(tot)
        if diagonal:
            key = jnp.where(col0 + col_id <= row_id, key, INT_MIN)
        keys_ref[:, pl.ds(col0, ck)] = key
        u = key.T ^ INT_MIN
        for blk in range(ck // PLANE_KEYS):
            slabs = [u[blk * PLANE_KEYS + 8 * k:blk * PLANE_KEYS + 8 * k + 8, :] for k in range(32)]
            slabs = _bit_transpose32(slabs)
            g0 = pl.multiple_of(c * (ck // 32) + blk * 8, 8)
            for o in range(32):
                planes_ref[o, pl.ds(g0, 8), :] = slabs[o]
        return carry

    @pl.when((pl.program_id(0) == 0) & (i == 0))
    def _():
        planes_ref[...] = jnp.zeros(planes_ref.shape, I32)

    lax.fori_loop(0, n_chunks - 1, functools.partial(score_body, diagonal=False), 0)
    score_body(n_chunks - 1, 0)

    n_groups = planes_ref.shape[1]
    group_id = lax.broadcasted_iota(I32, (n_groups, tq), 0)
    alive_ref[...] = jnp.where(group_id < n_chunks * (ck // 32), -1, 0)

    def ones_per_row(words):
        cnt = lax.population_count(words).reshape(n_groups // 8, 8, tq)
        return jnp.sum(jnp.sum(cnt, axis=0), axis=0, keepdims=True)

    def bit_step(it, carry):
        tau_u, n_above = carry
        alive = alive_ref[...]
        ones = alive & planes_ref[it]
        c1 = ones_per_row(ones)
        accept = (n_above + c1) >= topk
        alive_ref[...] = jnp.where(accept, ones, alive ^ ones)
        bit = lax.shift_left(jnp.int32(1), 31 - it)
        return jnp.where(accept, tau_u | bit, tau_u), jnp.where(accept, n_above, n_above + c1)

    zero = jnp.zeros((1, tq), I32)
    tau_u, n_gt = lax.fori_loop(0, 32, bit_step, (zero, zero))
    tau = tau_u ^ INT_MIN
    n_ge = jnp.where(tau > INT_MIN, n_gt + ones_per_row(alive_ref[...]), 0)

    def per_row(x):
        return jnp.broadcast_to(x, (LANES, tq)).T

    tau_lanes = jnp.maximum(tau, INT_MIN + 1)
    tau_sel = per_row(tau_lanes)
    has_ties = jnp.max(n_ge) > topk

    m_ref[...] = jnp.full(m_ref.shape, MASKED_MAX, F32)
    l_ref[...] = jnp.zeros(l_ref.shape, F32)
    acc_ref[...] = jnp.zeros(acc_ref.shape, F32)

    def attend(ranked):
        tau_t = jnp.tile(tau_sel, (1, n_sub))
        if ranked:
            need_eq = per_row((topk - n_gt).astype(F32))[:, 0:1]

        def body(c, eq_seen):
            col0 = pl.multiple_of(c * ck, ck)
            blk = keys_ref[:, pl.ds(col0, ck)]
            if ranked:
                eq = blk == tau_t
                eq_f = jnp.where(eq, 1.0, 0.0)
                before = _dot(eq_f.astype(BF16), tri_ref[...]) + eq_seen
                rank = jnp.where(eq, before, jnp.where(blk > tau_t, -1.0, jnp.inf))
                bias_ref[...] = jnp.where(rank < need_eq, 0.0, -jnp.inf)
                eq_seen = eq_seen + jnp.sum(eq_f, axis=1, keepdims=True)
            else:
                bias_ref[...] = jnp.where(blk >= tau_t, 0.0, -jnp.inf)
            for g in range(rows_q // ATT_ROWS):
                rows = slice(g * ATT_ROWS, (g + 1) * ATT_ROWS)
                off = (g * ATT_ROWS) % tq
                lg_ref[rows, :] = (_dot(aq_ref[rows, :], kt_ref[0, :, pl.ds(col0, ck)])
                                   + bias_ref[off:off + ATT_ROWS, :])
                m_old = m_ref[rows, :]
                m_new = jnp.maximum(m_old, jnp.max(lg_ref[rows, :], axis=1, keepdims=True))
                m_ref[rows, :] = m_new
                alpha = jnp.exp2(m_old - m_new)
                p = jnp.exp2(lg_ref[rows, :] - jnp.tile(m_new, (1, n_sub)))
                p_ref[rows, :] = p.astype(BF16)
                l_ref[rows, :] = alpha * l_ref[rows, :] + jnp.sum(p, axis=1, keepdims=True)
                acc_ref[rows, :] = (jnp.tile(alpha, (1, KV_WIDTH // LANES)) * acc_ref[rows, :]
                                    + _dot(p_ref[rows, :], v_ref[0, pl.ds(col0, ck), :]))
            return eq_seen
        lax.fori_loop(0, n_chunks, body, jnp.zeros((tq, 1), F32))

    @pl.when(has_ties)
    def _():
        attend(True)

    @pl.when(jnp.logical_not(has_ties))
    def _():
        attend(False)

    out = jnp.zeros((tq, ATT_WIDTH), F32)
    for h in range(N_HEADS):
        rows = slice(h * tq, (h + 1) * tq)
        o_h = acc_ref[rows, :] / jnp.tile(l_ref[rows, :], (1, KV_WIDTH // LANES))
        out = out + _dot(o_h.astype(BF16), sel_ref[h])
    o_ref[0] = out.astype(BF16)


def _dsa_prompt(qx, qc, wi, kct, kt, vb, sel):
    b, t, _ = qx.shape
    tq, ck = min(Q_TILE, t), min(K_CHUNK, t)
    assert t % tq == 0 and t % ck == 0 and ck % tq == 0
    topk = min(TOPK_MAX, t // 4)
    tri = jnp.asarray(np.triu(np.ones((ck, ck), np.float32), 1), BF16)
    tile = lambda width: pl.BlockSpec((1, tq, width), lambda i, j: (i, j, 0))
    whole = lambda rows, cols: pl.BlockSpec((1, rows, cols), lambda i, j: (i, 0, 0),
                                            pipeline_mode=pl.Buffered(1))
    return pl.pallas_call(
        functools.partial(_dsa_prompt_kernel, tq=tq, ck=ck, topk=topk),
        grid=(b, t // tq),
        in_specs=[tile(N_HEADS * MXU_DIM), tile(IDX_HEADS * MXU_DIM), tile(LANES),
                  whole(MXU_DIM, t), whole(MXU_DIM, t), whole(t, KV_WIDTH),
                  _resident(tri.shape, 2), _resident(sel.shape, 2)],
        out_specs=tile(ATT_WIDTH),
        out_shape=jax.ShapeDtypeStruct((b, t, ATT_WIDTH), BF16),
        scratch_shapes=[pltpu.VMEM((tq, t), I32),
                        pltpu.VMEM((32, t // 32, tq), I32),
                        pltpu.VMEM((t // 32, tq), I32),
                        pltpu.VMEM((N_HEADS * tq, MXU_DIM), BF16),
                        pltpu.VMEM((IDX_HEADS * tq, MXU_DIM), BF16),
                        pltpu.VMEM((IDX_HEADS, tq, LANES), F32),
                        pltpu.VMEM((N_HEADS * tq, LANES), F32),
                        pltpu.VMEM((N_HEADS * tq, LANES), F32),
                        pltpu.VMEM((N_HEADS * tq, KV_WIDTH), F32),
                        pltpu.VMEM((N_HEADS * tq, ck), F32),
                        pltpu.VMEM((N_HEADS * tq, ck), BF16),
                        pltpu.VMEM((tq, ck), F32)],
        compiler_params=_params(2),
        name="dsa_prompt",
    )(qx, qc, wi, kct, kt, vb, tri, sel)


def _dsa_sample_kernel(pt_ref, qx_ref, qc_ref, wb_ref, kcn_ref, kn_ref, vn_ref, tri_ref,
                       cik_hbm, ck_hbm, cv_hbm, o_ref,
                       ik_buf, k_buf, v_buf, keys_ref, bias_ref, lg_ref, sem,
                       *, layer, n_pages, page, chunk, tn, topk):
    b = pl.program_id(0)
    nb = pl.num_programs(0)
    past = n_pages * page
    width = past + LANES
    rows = tn * N_HEADS

    def copies(sample, slot, p):
        pg = pt_ref[sample, p]
        dst = pl.ds(pl.multiple_of(p * page, page), page)
        return (pltpu.make_async_copy(cik_hbm.at[layer, pg], ik_buf.at[slot, :, dst], sem.at[slot, 0]),
                pltpu.make_async_copy(ck_hbm.at[layer, pg], k_buf.at[slot, :, dst], sem.at[slot, 1]),
                pltpu.make_async_copy(cv_hbm.at[layer, pg], v_buf.at[slot, :, dst], sem.at[slot, 2]))

    def fetch(sample, slot):
        def body(p, carry):
            for cp in copies(sample, slot, p):
                cp.start()
            return carry
        lax.fori_loop(0, n_pages, body, 0)

    def wait(sample, slot):
        def body(p, carry):
            for cp in copies(sample, slot, p):
                cp.wait()
            return carry
        lax.fori_loop(0, n_pages, body, 0)

    slot = b % 2

    @pl.when(b == 0)
    def _():
        fetch(0, 0)

    @pl.when(b + 1 < nb)
    def _():
        fetch(b + 1, 1 - slot)

    wait(b, slot)

    qc = qc_ref[0]
    qx = qx_ref[0]
    wb = wb_ref[0]
    tok = lax.broadcasted_iota(I32, (tn, LANES), 0)
    col = lax.broadcasted_iota(I32, (tn, LANES), 1)

    def head_sum(s):
        n = s.shape[1]
        r = jnp.maximum(s, 0.0) * jnp.tile(wb, (1, n // LANES))
        return jnp.sum(r.reshape(tn, N_HEADS, n), axis=1)

    keys_ref[...] = jnp.full(keys_ref.shape, INT_MIN, I32)

    zero_rows = jnp.zeros((MXU_DIM - 3 * IDX_DIM, chunk), BF16)
    for c in range(past // chunk):
        cols = slice(c * chunk, (c + 1) * chunk)
        hi, lo = _split2(ik_buf[slot, :, cols])
        kct = jnp.concatenate([hi, hi, lo, zero_rows], axis=0)
        keys_ref[0:tn, cols] = _sortable_key(head_sum(_dot(qc, kct)))
    s_new = head_sum(_dot(qc, kcn_ref[0]))
    keys_ref[0:tn, past:width] = jnp.where(col <= tok, _sortable_key(s_new), INT_MIN)

    def count_ge(cand):
        hit = jnp.where(keys_ref[...] >= cand, 1, 0)
        parts = [hit[:, u * LANES:(u + 1) * LANES] for u in range(width // LANES)]
        while len(parts) > 1:
            parts = [a + b for a, b in zip(parts[0::2], parts[1::2])] + parts[len(parts) & ~1:]
        return jnp.sum(parts[0], axis=1, keepdims=True)

    tau, n_ge = _kth_largest(count_ge, topk, (8, 1))
    tau_sel = jnp.maximum(tau, INT_MIN + 1)
    has_ties = jnp.max(n_ge) > topk

    @pl.when(jnp.logical_not(has_ties))
    def _():
        bias_ref[...] = jnp.where(keys_ref[...] >= tau_sel, 0.0, -jnp.inf)

    @pl.when(has_ties)
    def _():
        n_gt = jnp.sum(jnp.where(keys_ref[...] > tau_sel, 1, 0), axis=1, keepdims=True)
        need_eq = (topk - n_gt).astype(F32)

        def body(u, eq_seen):
            c0 = pl.multiple_of(u * LANES, LANES)
            blk = keys_ref[:, pl.ds(c0, LANES)]
            eq = blk == tau_sel
            eq_f = jnp.where(eq, 1.0, 0.0)
            before = _dot(eq_f.astype(BF16), tri_ref[...]) + eq_seen
            rank = jnp.where(eq, before, jnp.where(blk > tau_sel, -1.0, jnp.inf))
            bias_ref[:, pl.ds(c0, LANES)] = jnp.where(rank < need_eq, 0.0, -jnp.inf)
            return eq_seen + jnp.sum(eq_f, axis=1, keepdims=True)

        lax.fori_loop(0, width // LANES, body, jnp.zeros((8, 1), F32))

    def expand(x):
        n = x.shape[1]
        return jnp.broadcast_to(x[0:tn].reshape(tn, 1, n), (tn, N_HEADS, n)).reshape(rows, n)

    for c in range(past // chunk):
        cols = slice(c * chunk, (c + 1) * chunk)
        lg_ref[:, cols] = _dot(qx, k_buf[slot, :, cols].astype(BF16)) + expand(bias_ref[:, cols])
    lg_ref[:, past:width] = _dot(qx, kn_ref[0]) + expand(bias_ref[:, past:width])
    lg = lg_ref[...]
    p = jnp.exp2(lg - jnp.max(lg, axis=1, keepdims=True))
    l = jnp.sum(p, axis=1, keepdims=True)
    lg_ref[...] = p
    acc = _dot_nt(lg_ref[:, past:width].astype(BF16), vn_ref[0])
    for c in range(past // chunk):
        cols = slice(c * chunk, (c + 1) * chunk)
        acc = acc + _dot_nt(lg_ref[:, cols].astype(BF16), v_buf[slot, :, cols].astype(BF16))
    o_ref[0] = acc / l


def _dsa_sample(page_table, qx, qc, wb, kcn, kn, vn, cache_ik, cache_k, cache_v, layer):
    db, rows, _ = qx.shape
    tn = rows // N_HEADS
    n_pages = page_table.shape[1]
    page = cache_ik.shape[3]
    past = n_pages * page
    chunk = min(SAMPLE_CHUNK, past)
    assert page == LANES and tn <= 8 and past % chunk == 0
    tri = jnp.asarray(np.triu(np.ones((page, page), np.float32), 1), BF16)
    topk = min(TOPK_MAX, (past + tn) // 4)
    per = lambda r, c: pl.BlockSpec((1, r, c), lambda i, pt: (i, 0, 0))
    grid_spec = pltpu.PrefetchScalarGridSpec(
        num_scalar_prefetch=1,
        grid=(db,),
        in_specs=[per(rows, MXU_DIM), per(rows, MXU_DIM), per(rows, LANES),
                  per(MXU_DIM, LANES), per(KV_WIDTH, LANES), per(KV_WIDTH, LANES),
                  pl.BlockSpec(tri.shape, lambda i, pt: (0, 0)),
                  pl.BlockSpec(memory_space=pl.ANY), pl.BlockSpec(memory_space=pl.ANY),
                  pl.BlockSpec(memory_space=pl.ANY)],
        out_specs=per(rows, KV_WIDTH),
        scratch_shapes=[pltpu.VMEM((2, IDX_DIM, past), F32),
                        pltpu.VMEM((2, KV_WIDTH, past), F32),
                        pltpu.VMEM((2, KV_WIDTH, past), F32),
                        pltpu.VMEM((8, past + LANES), I32),
                        pltpu.VMEM((8, past + LANES), F32),
                        pltpu.VMEM((rows, past + LANES), F32),
                        pltpu.SemaphoreType.DMA((2, 3))],
    )
    return pl.pallas_call(
        functools.partial(_dsa_sample_kernel, layer=layer, n_pages=n_pages, page=page, chunk=chunk,
                          tn=tn, topk=topk),
        grid_spec=grid_spec,
        out_shape=jax.ShapeDtypeStruct((db, rows, KV_WIDTH), F32),
        compiler_params=_params(1),
        name="dsa_sample",
    )(page_table, qx, qc, wb, kcn, kn, vn, tri, cache_ik, cache_k, cache_v)


def _merge_ffn_kernel(x_ref, att_ref, rec_ref, pg_ref, wa_ref, wh_ref, wo_ref,
                      g_ref, wg_ref, wu_ref, wd_ref, o_ref):
    d = x_ref.shape[1]
    pg = pg_ref[...]
    merged = (_sigmoid(pg[:, :d]) * _dot(att_ref[...], wa_ref[...])
              + _sigmoid(pg[:, d:]) * _dot(rec_ref[...], wh_ref[...]))
    x = x_ref[...] + _dot(merged.astype(BF16), wo_ref[...])
    o_ref[...] = _ffn_apply(x, g_ref, wg_ref, wu_ref, wd_ref)


def _merge_ffn(x, att, rec, pg, wa, wh, wo, norm, wg, wu, wd):
    n, d = x.shape
    tm = min(ROW_TILE, n)
    assert n % tm == 0 and wg.shape[1] % FFN_CHUNK == 0
    row = lambda width: pl.BlockSpec((tm, width), lambda i: (i, 0))
    weights = [wa, wh, wo, norm.reshape(1, d), wg, wu, wd]
    return pl.pallas_call(
        _merge_ffn_kernel,
        grid=(n // tm,),
        in_specs=[row(d), row(ATT_WIDTH), row(HG_WIDTH), row(2 * d)] + [_resident(a.shape, 1) for a in weights],
        out_specs=row(d),
        out_shape=jax.ShapeDtypeStruct((n, d), F32),
        compiler_params=_params(1),
        name="merge_ffn",
    )(x, att, rec, pg, *weights)


def _prep_layer(l, ffn1_norm, ffn1_w_gate, ffn1_w_up, ffn1_w_down, mix_norm, w_in, q_norm, k_norm, hg_norm,
                w_branch_attn, w_branch_hgrn, w_out, ffn2_norm, ffn2_w_gate, ffn2_w_up, ffn2_w_down):
    d = w_in.shape[1]
    w = w_in[l]
    off = [0]

    def take(width):
        a = w[:, off[0]:off[0] + width]
        off[0] += width
        return a

    wq, wk, wv = take(ATT_WIDTH), take(KV_WIDTH), take(KV_WIDTH)
    wqi, wki, wwi = take(IDX_HEADS * IDX_DIM), take(IDX_DIM), take(IDX_HEADS)
    wh = take(4 * HG_WIDTH)
    wg = take(2 * d)
    zeros = jnp.zeros((d, IDX_DIM), F32)
    wqi_heads = wqi.reshape(d, IDX_HEADS, IDX_DIM)
    wqi_arr = jnp.concatenate([wqi_heads, wqi_heads, wqi_heads, jnp.zeros_like(wqi_heads)], axis=2)
    wki_arr = jnp.concatenate([wki, wki, wki, zeros], axis=1)
    wwi_arr = jnp.concatenate([wwi, jnp.zeros((d, LANES - IDX_HEADS), F32)], axis=1)
    bd = np.kron(np.eye(ATT_WIDTH // HEAD_DIM, dtype=np.float32), np.ones((HEAD_DIM, HEAD_DIM), np.float32))
    pq = np.zeros((ATT_WIDTH, N_HEADS * MXU_DIM), np.float32)
    sel = np.zeros((N_HEADS, KV_WIDTH, ATT_WIDTH), np.float32)
    for h in range(N_HEADS):
        g = h // (N_HEADS // KV_HEADS)
        for e in range(HEAD_DIM):
            pq[h * HEAD_DIM + e, h * MXU_DIM + g * HEAD_DIM + e] = 1.0
            sel[h, g * HEAD_DIM + e, h * HEAD_DIM + e] = 1.0
    bf = lambda a: a.astype(BF16)
    return dict(
        ffn1=(ffn1_norm[l], bf(ffn1_w_gate[l]), bf(ffn1_w_up[l]), bf(ffn1_w_down[l])),
        ffn2=(ffn2_norm[l], bf(ffn2_w_gate[l]), bf(ffn2_w_up[l]), bf(ffn2_w_down[l])),
        proj=dict(mix_norm=mix_norm[l].reshape(1, d), wq=bf(wq), wk=bf(wk), wv=bf(wv),
                  wqi=bf(wqi_arr.reshape(d, IDX_HEADS * MXU_DIM)), wki=bf(wki_arr), wwi=bf(wwi_arr),
                  wh=bf(wh), wg=bf(wg), bd=jnp.asarray(bd, BF16), pq=jnp.asarray(pq, BF16),
                  qn=jnp.tile(q_norm[l], N_HEADS).reshape(1, ATT_WIDTH),
                  kn=jnp.tile(k_norm[l], KV_HEADS).reshape(1, KV_WIDTH)),
        sel=jnp.asarray(sel, BF16),
        hg_norm=hg_norm[l],
        merge=(bf(w_branch_attn[l]), bf(w_branch_hgrn[l]), bf(w_out[l])),
    )


def kernel(x_prompt, x_sample, cache_k, cache_v, cache_idx_k, state_hgrn, page_table, hg_lower_bound,
           ffn1_norm, ffn1_w_gate, ffn1_w_up, ffn1_w_down, mix_norm, w_in, q_norm, k_norm, hg_norm,
           w_branch_attn, w_branch_hgrn, w_out, ffn2_norm, ffn2_w_gate, ffn2_w_up, ffn2_w_down):
    b, t, d = x_prompt.shape
    db, tn, _ = x_sample.shape
    depth = w_in.shape[0]
    n_pool, page = cache_k.shape[1], cache_k.shape[2]
    past = page_table.shape[1] * page
    cache_ikt = cache_idx_k.transpose(0, 1, 3, 2)
    cache_kt = cache_k.transpose(0, 1, 3, 4, 2).reshape(depth, n_pool, KV_WIDTH, page)
    cache_vt = cache_v.transpose(0, 1, 3, 4, 2).reshape(depth, n_pool, KV_WIDTH, page)
    lb_param = hg_lower_bound.astype(F32)
    tn_pad = -(-tn // HG_SUB) * HG_SUB

    xp = x_prompt.reshape(b * t, d)
    xs = x_sample.reshape(db * tn, d)
    outs = [[] for _ in range(8)]
    for l in range(depth):
        w = _prep_layer(l, ffn1_norm, ffn1_w_gate, ffn1_w_up, ffn1_w_down, mix_norm, w_in, q_norm, k_norm,
                        hg_norm, w_branch_attn, w_branch_hgrn, w_out, ffn2_norm, ffn2_w_gate, ffn2_w_up,
                        ffn2_w_down)
        xp = _ffn(xp, *w["ffn1"])
        qx, k, v, vb, qc, ki, wi, ph, pg, kt, kct = _proj(xp, b, t, 0, w["proj"], transposed=True)
        att = _dsa_prompt(qx.reshape(b, t, -1), qc.reshape(b, t, -1), wi.reshape(b, t, -1),
                          kct, kt, vb.reshape(b, t, -1), w["sel"])
        rec, s_p = _hgrn(ph.reshape(b, t, -1), lb_param, jnp.zeros((b, HG_HEADS, HG_KDIM, HG_VDIM), F32),
                         w["hg_norm"], l, t)
        xp = _merge_ffn(xp, att.reshape(b * t, -1), rec.reshape(b * t, -1), pg, *w["merge"], *w["ffn2"])
        outs[0].append(k.reshape(b, t, KV_HEADS, HEAD_DIM))
        outs[1].append(v.reshape(b, t, KV_HEADS, HEAD_DIM))
        outs[2].append(ki.reshape(b, t, IDX_DIM))
        outs[3].append(s_p.astype(x_prompt.dtype))

        xs = _ffn(xs, *w["ffn1"])
        qx, k, v, vb, qc, ki, wi, ph, pg, kb, kc = _proj(xs, db, tn, past, w["proj"], transposed=False)
        rows = tn * N_HEADS
        qx3 = qx.reshape(db, rows, MXU_DIM)
        qc3 = qc.reshape(db, rows, MXU_DIM)
        wb3 = jnp.broadcast_to(wi[:, :IDX_HEADS].reshape(db, rows, 1), (db, rows, LANES))
        pad_new = lambda a: jnp.pad(a.reshape(db, tn, -1), ((0, 0), (0, page - tn), (0, 0))).transpose(0, 2, 1)
        o_s = _dsa_sample(page_table, qx3, qc3, wb3, pad_new(kc), pad_new(kb), pad_new(vb),
                          cache_ikt, cache_kt, cache_vt, l)
        o_s = o_s.reshape(db, tn, KV_HEADS, N_HEADS // KV_HEADS, KV_HEADS, HEAD_DIM)
        att_s = jnp.stack([o_s[:, :, g, :, g, :] for g in range(KV_HEADS)], axis=2)
        att_s = att_s.reshape(db * tn, ATT_WIDTH).astype(BF16)
        ph3 = jnp.pad(ph.reshape(db, tn, -1), ((0, 0), (0, tn_pad - tn), (0, 0)))
        rec, s_s = _hgrn(ph3, lb_param, state_hgrn[l], w["hg_norm"], l, tn)
        rec = rec[:, :tn].reshape(db * tn, HG_WIDTH)
        xs = _merge_ffn(xs, att_s, rec, pg, *w["merge"], *w["ffn2"])
        outs[4].append(k.reshape(db, tn, KV_HEADS, HEAD_DIM))
        outs[5].append(v.reshape(db, tn, KV_HEADS, HEAD_DIM))
        outs[6].append(ki.reshape(db, tn, IDX_DIM))
        outs[7].append(s_s.astype(state_hgrn.dtype))

    return (xp.reshape(b, t, d), xs.reshape(db, tn, d)) + tuple(jnp.stack(o) for o in outs)
```
